```python
import math
import jax, jax.numpy as jnp
from jax import lax
import numpy as np

D_MODEL = 2048
BATCH = 2
SEQ = 8192
DEPTH = 4

HEAD_DIM = 64
ROPE_THETA = 10000.0
ATTN_BLOCK = 128
LN_EPS = 1e-5

SWA_Q_HEADS = 16
SWA_KV_HEADS = 2
SWA_WINDOW = 128

SSD_HEADS = 24
SSD_HEADDIM = 64
SSD_INNER = SSD_HEADS * SSD_HEADDIM
SSD_GROUPS = 4
SSD_STATE = 128
SSD_CONV = 4
SSD_CHUNK = 128
SSD_CONV_DIM = SSD_INNER + 2 * SSD_GROUPS * SSD_STATE

DIL_CONFIGS = ((128, 1), (512, 4), (2048, 16))
DIL_HEADS_PER_GROUP = 6
DIL_HEADS = DIL_HEADS_PER_GROUP * len(DIL_CONFIGS)

N_BRANCH = 3
A_Q = SWA_Q_HEADS * HEAD_DIM
A_KV = SWA_KV_HEADS * HEAD_DIM
C_QKV = DIL_HEADS * HEAD_DIM
C_OUT = DIL_HEADS_PER_GROUP * HEAD_DIM
SPLIT_SIZES = (A_Q, A_KV, A_KV, SSD_INNER, SSD_CONV_DIM, SSD_HEADS, C_QKV, C_QKV, C_QKV, N_BRANCH * D_MODEL)
IN_WIDTH = sum(SPLIT_SIZES)

N_GROUPS = 4
EXPERTS_PER_GROUP = 8
N_EXPERTS = N_GROUPS * EXPERTS_PER_GROUP
TOP_K = 2
D_EXPERT = 512
MOE_BLOCK = 128

DEEPNORM_ALPHA = (2 * DEPTH) ** 0.25
DEEPNORM_BETA = (8 * DEPTH) ** -0.25

kernel_name = 'hybrid_swa_ssd_dilated_hmoe_deepnorm'


def rope_tables(seq_len):
    inv_freq = 1.0 / (ROPE_THETA ** (jnp.arange(0, HEAD_DIM, 2, dtype=jnp.float32) / HEAD_DIM))
    ang = jnp.arange(seq_len, dtype=jnp.float32)[:, None] * inv_freq[None, :]
    return jnp.cos(ang), jnp.sin(ang)


def apply_rope(t, cos, sin):
    t1, t2 = jnp.split(t, 2, axis=-1)
    c, s = cos.astype(t.dtype), sin.astype(t.dtype)
    return jnp.concatenate([t1 * c - t2 * s, t2 * c + t1 * s], axis=-1)


def layer_norm(t, g, b):
    tf = t.astype(jnp.float32)
    mu = tf.mean(-1, keepdims=True)
    var = jnp.square(tf - mu).mean(-1, keepdims=True)
    y = (tf - mu) * lax.rsqrt(var + LN_EPS) * g.astype(jnp.float32) + b.astype(jnp.float32)
    return y.astype(t.dtype)


def split_columns(p, sizes):
    pts, acc = [], 0
    for s in sizes[:-1]:
        acc += s
        pts.append(acc)
    return jnp.split(p, pts, axis=-1)


def banded_attention(q, k, v, max_dist, sinks=None):
    b, hk, g, L, d = q.shape
    nb = -(-L // ATTN_BLOCK)
    pad = nb * ATTN_BLOCK - L
    q = jnp.pad(q, ((0, 0), (0, 0), (0, 0), (0, pad), (0, 0)))
    k = jnp.pad(k, ((0, 0), (0, 0), (0, pad), (0, 0)))
    v = jnp.pad(v, ((0, 0), (0, 0), (0, pad), (0, 0)))
    qb = q.reshape(b, hk, g, nb, ATTN_BLOCK, d)

    def with_prev(t):
        tb = t.reshape(b, hk, nb, ATTN_BLOCK, d)
        prev = jnp.pad(tb, ((0, 0), (0, 0), (1, 0), (0, 0), (0, 0)))[:, :, :-1]
        return jnp.concatenate([prev, tb], axis=3)

    kc, vc = with_prev(k), with_prev(v)
    s = jnp.einsum('bhgnqd,bhnkd->bhgnqk', qb, kc).astype(jnp.float32) * (d ** -0.5)
    qpos = ATTN_BLOCK + jnp.arange(ATTN_BLOCK)
    kpos = jnp.arange(2 * ATTN_BLOCK)
    dist = qpos[:, None] - kpos[None, :]
    in_band = ((dist >= 0) & (dist <= max_dist))[None]
    has_prev = (jnp.arange(nb)[:, None, None] > 0) | (kpos[None, None, :] >= ATTN_BLOCK)
    s = jnp.where(in_band & has_prev, s, -jnp.inf)
    m = s.max(-1)
    if sinks is not None:
        sink = sinks.astype(jnp.float32).reshape(1, hk, g, 1, 1)
        m = jnp.maximum(m, sink)
    p = jnp.exp(s - m[..., None])
    denom = p.sum(-1)
    if sinks is not None:
        denom = denom + jnp.exp(sink - m)
    o = jnp.einsum('bhgnqk,bhnkd->bhgnqd', p, vc.astype(jnp.float32)) / denom[..., None]
    lse = m + jnp.log(denom)
    o = o.reshape(b, hk, g, nb * ATTN_BLOCK, d)[:, :, :, :L].astype(v.dtype)
    lse = lse.reshape(b, hk, g, nb * ATTN_BLOCK)[:, :, :, :L]
    return o, lse


def swa_sink_attention(q, k, v, sinks, cos, sin):
    b, S, _ = q.shape
    g = SWA_Q_HEADS // SWA_KV_HEADS
    q = apply_rope(q.reshape(b, S, SWA_KV_HEADS, g, HEAD_DIM).transpose(0, 2, 3, 1, 4), cos, sin)
    k = apply_rope(k.reshape(b, S, SWA_KV_HEADS, HEAD_DIM).transpose(0, 2, 1, 3), cos, sin)
    v = v.reshape(b, S, SWA_KV_HEADS, HEAD_DIM).transpose(0, 2, 1, 3)
    o, _ = banded_attention(q, k, v, SWA_WINDOW - 1, sinks.reshape(SWA_KV_HEADS, g))
    return o.transpose(0, 3, 1, 2, 4).reshape(b, S, A_Q)


def dilated_attention(q, k, v, cos, sin):
    b, S, _ = q.shape
    H = DIL_HEADS_PER_GROUP

    def heads(t):
        return t.reshape(b, S, DIL_HEADS, HEAD_DIM).transpose(0, 2, 1, 3)

    q, k, v = apply_rope(heads(q), cos, sin), apply_rope(heads(k), cos, sin), heads(v)
    outs, lses = [], []
    for gi, (window, dil) in enumerate(DIL_CONFIGS):
        L = S // dil

        def by_stride(t):
            t = t[:, gi * H:(gi + 1) * H].reshape(b, H, L, dil, HEAD_DIM)
            return t.transpose(0, 1, 3, 2, 4).reshape(b, H * dil, L, HEAD_DIM)

        o, lse = banded_attention(by_stride(q)[:, :, None], by_stride(k), by_stride(v), window // dil)
        outs.append(o[:, :, 0].reshape(b, H, dil, L, HEAD_DIM).transpose(0, 1, 3, 2, 4).reshape(b, H, S, HEAD_DIM))
        lses.append(lse[:, :, 0].reshape(b, H, dil, L).transpose(0, 1, 3, 2).reshape(b, H, S))
    w = jax.nn.softmax(jnp.stack(lses), axis=0)
    y = (jnp.stack(outs).astype(jnp.float32) * w[..., None]).sum(0)
    return y.transpose(0, 2, 1, 3).reshape(b, S, C_OUT).astype(q.dtype)


def causal_depthwise_conv(t, w, bias):
    c = t.shape[-1]
    y = lax.conv_general_dilated(t, w.astype(t.dtype)[:, None, :], window_strides=(1,),
                                 padding=((SSD_CONV - 1, 0),),
                                 dimension_numbers=('NWC', 'WIO', 'NWC'),
                                 feature_group_count=c)
    return y + bias.astype(t.dtype)


def ssd_mixer(z, xbc, dt_raw, conv_w, conv_b, dt_bias, a_log, d_skip, norm_w):
    b, S, _ = xbc.shape
    nc, Q = S // SSD_CHUNK, SSD_CHUNK
    E = SSD_HEADS // SSD_GROUPS
    xbc = jax.nn.silu(causal_depthwise_conv(xbc, conv_w, conv_b))
    xs, bm, cm = jnp.split(xbc, [SSD_INNER, SSD_INNER + SSD_GROUPS * SSD_STATE], axis=-1)
    dt = jax.nn.softplus(dt_raw.astype(jnp.float32) + dt_bias.astype(jnp.float32))
    a = -jnp.exp(a_log.astype(jnp.float32)).reshape(SSD_GROUPS, E)
    x = xs.astype(jnp.float32).reshape(b, nc, Q, SSD_GROUPS, E, SSD_HEADDIM)
    bc = bm.astype(jnp.float32).reshape(b, nc, Q, SSD_GROUPS, SSD_STATE)
    cc = cm.astype(jnp.float32).reshape(b, nc, Q, SSD_GROUPS, SSD_STATE)
    dt = dt.reshape(b, nc, Q, SSD_GROUPS, E)
    xdt = x * dt[..., None]
    a_cs = jnp.cumsum((dt * a).transpose(0, 1, 3, 4, 2), axis=-1)
    causal = jnp.tril(jnp.ones((Q, Q), dtype=bool))
    decay_in = jnp.exp(jnp.where(causal, a_cs[..., :, None] - a_cs[..., None, :], -jnp.inf))
    cb = jnp.einsum('bclgn,bcsgn->bcgls', cc, bc)
    y_diag = jnp.einsum('bcgels,bcsgep->bclgep', cb[:, :, :, None] * decay_in, xdt)
    decay_to_end = jnp.exp(a_cs[..., -1:] - a_cs)
    chunk_states = jnp.einsum('bcsgn,bcges,bcsgep->bcgepn', bc, decay_to_end, xdt)
    chunk_decay = jnp.exp(a_cs[..., -1])

    def step(h, inp):
        st, dec = inp
        return h * dec[..., None, None] + st, h

    h0 = jnp.zeros((b, SSD_GROUPS, E, SSD_HEADDIM, SSD_STATE), jnp.float32)
    _, h_prev = lax.scan(step, h0, (jnp.moveaxis(chunk_states, 1, 0), jnp.moveaxis(chunk_decay, 1, 0)))
    h_prev = jnp.moveaxis(h_prev, 0, 1)
    y_off = jnp.einsum('bclgn,bcgepn,bcgel->bclgep', cc, h_prev, jnp.exp(a_cs))
    y = y_diag + y_off + x * d_skip.astype(jnp.float32).reshape(SSD_GROUPS, E, 1)
    y = y.reshape(b, S, SSD_INNER) * jax.nn.silu(z.astype(jnp.float32))
    yg = y.reshape(b, S, SSD_GROUPS, SSD_INNER // SSD_GROUPS)
    yg = yg * lax.rsqrt(jnp.square(yg).mean(-1, keepdims=True) + LN_EPS)
    return (yg.reshape(b, S, SSD_INNER) * norm_w.astype(jnp.float32)).astype(z.dtype)


def hierarchical_moe(x, wg, bg, we, be, w1, w3, w2):
    b, S, dm = x.shape
    T = b * S
    xt = x.reshape(T, dm)
    g_logits = (xt @ wg).astype(jnp.float32) + bg.astype(jnp.float32)
    g_sel = jnp.argmax(g_logits, axis=-1).astype(jnp.int32)
    p_group = jnp.take_along_axis(jax.nn.softmax(g_logits, axis=-1), g_sel[:, None], axis=1)[:, 0]
    e_logits = jnp.einsum('td,gde->tge', xt, we).astype(jnp.float32) + be.astype(jnp.float32)
    e_logits = jnp.take_along_axis(e_logits, g_sel[:, None, None], axis=1)[:, 0]
    top_v, top_i = lax.top_k(e_logits, TOP_K)
    combine = (jax.nn.softmax(top_v, axis=-1) * p_group[:, None]).astype(x.dtype)
    expert_ids = g_sel[:, None] * EXPERTS_PER_GROUP + top_i.astype(jnp.int32)

    A_ = T * TOP_K
    flat_e = expert_ids.reshape(A_)
    order = jnp.argsort(flat_e)
    sorted_e = flat_e[order]
    counts = jnp.zeros((N_EXPERTS,), jnp.int32).at[flat_e].add(1)
    padded = (counts + MOE_BLOCK - 1) // MOE_BLOCK * MOE_BLOCK
    pad_end = jnp.cumsum(padded)
    pad_start = pad_end - padded
    start = jnp.cumsum(counts) - counts
    dest = pad_start[sorted_e] + jnp.arange(A_, dtype=jnp.int32) - start[sorted_e]
    n_blocks = -(-A_ // MOE_BLOCK) + N_EXPERTS
    tok_of_assign = order // TOP_K
    slot_tok = jnp.full((n_blocks * MOE_BLOCK,), T, jnp.int32).at[dest].set(tok_of_assign)
    x_pad = jnp.concatenate([xt, jnp.zeros((1, dm), xt.dtype)], axis=0)
    xb = x_pad[slot_tok].reshape(n_blocks, MOE_BLOCK, dm)
    block_e = jnp.minimum(jnp.searchsorted(pad_end, jnp.arange(n_blocks, dtype=jnp.int32) * MOE_BLOCK, side='right'),
                          N_EXPERTS - 1)

    def expert_block(args):
        xblk, e = args
        h = jax.nn.silu(xblk @ w1[e]) * (xblk @ w3[e])
        return h @ w2[e]

    yb = lax.map(expert_block, (xb, block_e)).reshape(n_blocks * MOE_BLOCK, dm)
    y_assign = yb[dest] * combine.reshape(A_)[order][:, None]
    y = jax.ops.segment_sum(y_assign, tok_of_assign, num_segments=T)
    return y.reshape(b, S, dm).astype(x.dtype)


def setup_inputs(seed: int = 0) -> dict:
    key = jax.random.key(seed)
    ks = jax.random.split(key, 25)
    f32 = jnp.float32
    L = DEPTH

    def nrm(k, shape, scale):
        return jax.random.normal(k, shape, f32) * scale

    dt_init = jnp.exp(jax.random.uniform(ks[6], (L, SSD_HEADS), f32, math.log(1e-3), math.log(1e-1)))
    return {
        'x': nrm(ks[0], (BATCH, SEQ, D_MODEL), 1.0),
        'w_in': nrm(ks[1], (L, D_MODEL, IN_WIDTH), D_MODEL ** -0.5),
        'b_in': nrm(ks[2], (L, IN_WIDTH), 0.02),
        'attn_sinks': nrm(ks[3], (L, SWA_Q_HEADS), 0.5),
        'conv_w': nrm(ks[4], (L, SSD_CONV, SSD_CONV_DIM), 0.5),
        'conv_b': nrm(ks[5], (L, SSD_CONV_DIM), 0.02),
        'dt_bias': dt_init + jnp.log(-jnp.expm1(-dt_init)),
        'a_log': jnp.log(jax.random.uniform(ks[7], (L, SSD_HEADS), f32, 1.0, 16.0)),
        'd_skip': 1.0 + nrm(ks[8], (L, SSD_HEADS), 0.1),
        'ssd_norm_w': 1.0 + nrm(ks[9], (L, SSD_INNER), 0.1),
        'proj_swa': nrm(ks[10], (L, A_Q, D_MODEL), A_Q ** -0.5 * DEEPNORM_BETA),
        'proj_ssd': nrm(ks[11], (L, SSD_INNER, D_MODEL), SSD_INNER ** -0.5 * DEEPNORM_BETA),
        'proj_dil': nrm(ks[12], (L, C_OUT, D_MODEL), C_OUT ** -0.5 * DEEPNORM_BETA),
        'w_out': nrm(ks[13], (L, D_MODEL, D_MODEL), D_MODEL ** -0.5 * DEEPNORM_BETA),
        'ln1_g': 1.0 + nrm(ks[14], (L, D_MODEL), 0.1),
        'ln1_b': nrm(ks[15], (L, D_MODEL), 0.02),
        'router_group_w': nrm(ks[16], (L, D_MODEL, N_GROUPS), D_MODEL ** -0.5),
        'router_group_b': nrm(ks[17], (L, N_GROUPS), 0.01),
        'router_expert_w': nrm(ks[18], (L, N_GROUPS, D_MODEL, EXPERTS_PER_GROUP), D_MODEL ** -0.5),
        'router_expert_b': nrm(ks[19], (L, N_GROUPS, EXPERTS_PER_GROUP), 0.01),
        'expert_w1': nrm(ks[20], (L, N_EXPERTS, D_MODEL, D_EXPERT), D_MODEL ** -0.5),
        'expert_w3': nrm(ks[21], (L, N_EXPERTS, D_MODEL, D_EXPERT), D_MODEL ** -0.5),
        'expert_w2': nrm(ks[22], (L, N_EXPERTS, D_EXPERT, D_MODEL), D_EXPERT ** -0.5 * DEEPNORM_BETA),
        'ln2_g': 1.0 + nrm(ks[23], (L, D_MODEL), 0.1),
        'ln2_b': nrm(ks[24], (L, D_MODEL), 0.02),
    }


def reference(x, w_in, b_in, attn_sinks, conv_w, conv_b, dt_bias, a_log, d_skip, ssd_norm_w,
              proj_swa, proj_ssd, proj_dil, w_out, ln1_g, ln1_b,
              router_group_w, router_group_b, router_expert_w, router_expert_b,
              expert_w1, expert_w3, expert_w2, ln2_g, ln2_b):
    b, S, _ = x.shape
    cos, sin = rope_tables(S)
    for l in range(DEPTH):
        p = x @ w_in[l] + b_in[l]
        q_a, k_a, v_a, z, xbc, dt_raw, q_c, k_c, v_c, gate_logits = split_columns(p, SPLIT_SIZES)
        y_swa = swa_sink_attention(q_a, k_a, v_a, attn_sinks[l], cos, sin)
        y_ssd = ssd_mixer(z, xbc, dt_raw, conv_w[l], conv_b[l], dt_bias[l], a_log[l],
                          d_skip[l], ssd_norm_w[l])
        y_dil = dilated_attention(q_c, k_c, v_c, cos, sin)
        gates = jax.nn.sigmoid(gate_logits.astype(jnp.float32)).astype(x.dtype).reshape(b, S, N_BRANCH, D_MODEL)
        merged = (gates[:, :, 0] * (y_swa @ proj_swa[l])
                  + gates[:, :, 1] * (y_ssd @ proj_ssd[l])
                  + gates[:, :, 2] * (y_dil @ proj_dil[l]))
        x = layer_norm(DEEPNORM_ALPHA * x + merged @ w_out[l], ln1_g[l], ln1_b[l])
        moe = hierarchical_moe(x, router_group_w[l], router_group_b[l], router_expert_w[l],
                               router_expert_b[l], expert_w1[l], expert_w3[l], expert_w2[l])
        x = layer_norm(DEEPNORM_ALPHA * x + moe, ln2_g[l], ln2_b[l])
    return x
```

```python
import functools

import jax
import jax.numpy as jnp
from jax import lax
from jax.experimental import pallas as pl
from jax.experimental.pallas import tpu as pltpu

F32 = jnp.float32
BF16 = jnp.bfloat16
I32 = jnp.int32

LANES = 128
SUBLANES = 8
VMEM_LIMIT_BYTES = 48 * 1024 * 1024

HEAD_DIM = 64
HALF_HEAD = HEAD_DIM // 2
ROPE_THETA = 10000.0
ATTN_BLOCK = 128
LN_EPS = 1e-5

SWA_Q_HEADS = 16
SWA_KV_HEADS = 2
SWA_WINDOW = 128

SSD_HEADS = 24
SSD_HEADDIM = 64
SSD_INNER = SSD_HEADS * SSD_HEADDIM
SSD_GROUPS = 4
SSD_STATE = 128
SSD_CONV = 4
SSD_CHUNK = 128
SSD_HEADS_PER_GROUP = SSD_HEADS // SSD_GROUPS
SSD_GROUP_WIDTH = SSD_INNER // SSD_GROUPS
SSD_BC_WIDTH = 2 * SSD_GROUPS * SSD_STATE

DIL_CONFIGS = ((128, 1), (512, 4), (2048, 16))
DIL_HEADS_PER_GROUP = 6
DIL_HEADS = DIL_HEADS_PER_GROUP * len(DIL_CONFIGS)

A_Q = SWA_Q_HEADS * HEAD_DIM
A_KV = SWA_KV_HEADS * HEAD_DIM
A_KV_DUP = 2 * A_KV
C_QKV = DIL_HEADS * HEAD_DIM
C_OUT = DIL_HEADS_PER_GROUP * HEAD_DIM

N_GROUPS = 4
EXPERTS_PER_GROUP = 8
N_EXPERTS = N_GROUPS * EXPERTS_PER_GROUP
TOP_K = 2
D_EXPERT = 512
MOE_BLOCK = 128

NEG_INF = float("-inf")


def _cparams(*sem):
    return pltpu.CompilerParams(dimension_semantics=sem, vmem_limit_bytes=VMEM_LIMIT_BYTES)


def _sigmoid(v):
    return 1.0 / (1.0 + jnp.exp(-v))


def _proj_kernel(*refs, rope_blocks, act):
    if rope_blocks:
        x_ref, w_ref, b_ref, cos_ref, sin_ref, o_ref = refs
    else:
        x_ref, w_ref, b_ref, o_ref = refs
    acc = jnp.dot(x_ref[...], w_ref[...], preferred_element_type=F32) + b_ref[...]
    if rope_blocks:
        cos = cos_ref[...]
        sin = sin_ref[...]
        lane = lax.broadcasted_iota(I32, cos.shape, 1)
        first_half = (lane % HEAD_DIM) < HALF_HEAD
        for c in range(acc.shape[1] // LANES):
            v = acc[:, c * LANES:(c + 1) * LANES]
            if c < rope_blocks:
                partner = jnp.where(first_half, pltpu.roll(v, LANES - HALF_HEAD, 1), pltpu.roll(v, HALF_HEAD, 1))
                v = v * cos + partner * sin
            o_ref[:, c * LANES:(c + 1) * LANES] = v.astype(o_ref.dtype)
    elif act == "sigmoid":
        o_ref[...] = _sigmoid(acc).astype(o_ref.dtype)
    else:
        o_ref[...] = acc.astype(o_ref.dtype)


def _proj(xb, w, b, out_dtype, tm, tn, seq, rope=None, rope_blocks=0, act=None, name="proj"):
    t, d = xb.shape
    n = w.shape[1]
    assert t % tm == 0 and n % tn == 0 and seq % tm == 0
    in_specs = [
        pl.BlockSpec((tm, d), lambda j, i: (i, 0)),
        pl.BlockSpec((d, tn), lambda j, i: (0, j)),
        pl.BlockSpec((1, tn), lambda j, i: (0, j)),
    ]
    args = [xb, w, b.reshape(1, n)]
    if rope_blocks:
        sblocks = seq // tm
        in_specs += [pl.BlockSpec((tm, LANES), lambda j, i: (i % sblocks, 0))] * 2
        args += list(rope)
    return pl.pallas_call(
        functools.partial(_proj_kernel, rope_blocks=rope_blocks, act=act),
        grid=(n // tn, t // tm),
        in_specs=in_specs,
        out_specs=pl.BlockSpec((tm, tn), lambda j, i: (i, j)),
        out_shape=jax.ShapeDtypeStruct((t, n), out_dtype),
        compiler_params=_cparams("arbitrary", "arbitrary"),
        name=name,
    )(*args)


def _band_attn_kernel(*refs, n_pairs, kv_block_of_pair, max_dist, has_sink, want_lse, tq, seq_axis):
    refs = list(refs)
    sink_ref = refs.pop(0) if has_sink else None
    q_ref, kc_ref, kp_ref, vc_ref, vp_ref, o_ref = refs[:6]
    l_ref = refs[6] if want_lse else None
    kf, vf = refs[-2:]
    n = pl.program_id(seq_axis)

    kf[0:ATTN_BLOCK, :] = kp_ref[...]
    kf[ATTN_BLOCK:, :] = kc_ref[...]
    vf[0:ATTN_BLOCK, :] = vp_ref[...]
    vf[ATTN_BLOCK:, :] = vc_ref[...]

    row = lax.broadcasted_iota(I32, (ATTN_BLOCK, 2 * ATTN_BLOCK), 0)
    col = lax.broadcasted_iota(I32, (ATTN_BLOCK, 2 * ATTN_BLOCK), 1)
    dist = row + ATTN_BLOCK - col
    in_band = (dist >= 0) & (dist <= max_dist)
    bias_any = jnp.where(in_band, 0.0, NEG_INF).astype(F32)
    bias_first = jnp.where(in_band & (col >= ATTN_BLOCK), 0.0, NEG_INF).astype(F32)
    lane = lax.broadcasted_iota(I32, (ATTN_BLOCK, LANES), 1)
    low = lane < HEAD_DIM
    scale = HEAD_DIM ** -0.5

    def sub_block(j, carry):
        r0 = pl.multiple_of(j * ATTN_BLOCK, ATTN_BLOCK)
        bias = jnp.where(jnp.logical_and(n == 0, j == 0), bias_first, bias_any)
        for hp in range(n_pairs):
            q2 = q_ref[pl.ds(r0, ATTN_BLOCK), hp * LANES:(hp + 1) * LANES]
            kb = kv_block_of_pair(hp)
            k2 = kf[pl.ds(r0, 2 * ATTN_BLOCK), kb * LANES:(kb + 1) * LANES]
            v2 = vf[pl.ds(r0, 2 * ATTN_BLOCK), kb * LANES:(kb + 1) * LANES]
            outs, lses = [], []
            for half in range(2):
                qh = jnp.where(low if half == 0 else jnp.logical_not(low), q2, jnp.zeros_like(q2))
                s = lax.dot_general(qh, k2, (((1,), (1,)), ((), ())), preferred_element_type=F32) * scale
                s = s + bias
                m = jnp.max(s, axis=-1, keepdims=True)
                if has_sink:
                    sk = sink_ref[2 * hp + half]
                    m = jnp.maximum(m, sk)
                p = jnp.exp(s - m)
                den = jnp.sum(p, axis=-1, keepdims=True)
                if has_sink:
                    den = den + jnp.exp(sk - m)
                o = jnp.dot(p.astype(BF16), v2, preferred_element_type=F32) / den
                outs.append(o)
                lses.append(m + jnp.log(den))
            o_ref[pl.ds(r0, ATTN_BLOCK), hp * LANES:(hp + 1) * LANES] = jnp.where(low, outs[0], outs[1]).astype(o_ref.dtype)
            if want_lse:
                l_ref[pl.ds(r0, ATTN_BLOCK), hp * LANES:(hp + 1) * LANES] = jnp.where(low, lses[0], lses[1])
        return carry

    lax.fori_loop(0, tq // ATTN_BLOCK, sub_block, 0)


def _attn_tile(length):
    return min(512, length)


def _swa_attention(qkv, sinks, batch, seq):
    tq = _attn_tile(seq)
    qkv = qkv.reshape(batch, seq, A_Q + 2 * A_KV_DUP)
    k_blk = A_Q // A_KV_DUP
    v_blk = k_blk + 1
    sub = tq // ATTN_BLOCK

    def prev(n):
        return jnp.maximum(n * sub - 1, 0)

    pairs_per_kv = SWA_Q_HEADS // SWA_KV_HEADS // 2
    kern = functools.partial(
        _band_attn_kernel, n_pairs=SWA_Q_HEADS // 2, kv_block_of_pair=lambda hp: hp // pairs_per_kv,
        max_dist=SWA_WINDOW - 1, has_sink=True, want_lse=False, tq=tq, seq_axis=1)
    return pl.pallas_call(
        kern,
        grid=(batch, seq // tq),
        in_specs=[
            pl.BlockSpec(memory_space=pltpu.SMEM),
            pl.BlockSpec((None, tq, A_Q), lambda b, n: (b, n, 0)),
            pl.BlockSpec((None, tq, A_KV_DUP), lambda b, n: (b, n, k_blk)),
            pl.BlockSpec((None, ATTN_BLOCK, A_KV_DUP), lambda b, n: (b, prev(n), k_blk)),
            pl.BlockSpec((None, tq, A_KV_DUP), lambda b, n: (b, n, v_blk)),
            pl.BlockSpec((None, ATTN_BLOCK, A_KV_DUP), lambda b, n: (b, prev(n), v_blk)),
        ],
        out_specs=pl.BlockSpec((None, tq, A_Q), lambda b, n: (b, n, 0)),
        out_shape=jax.ShapeDtypeStruct((batch, seq, A_Q), BF16),
        scratch_shapes=[pltpu.VMEM((tq + ATTN_BLOCK, A_KV_DUP), BF16), pltpu.VMEM((tq + ATTN_BLOCK, A_KV_DUP), BF16)],
        compiler_params=_cparams("arbitrary", "arbitrary"),
        name="swa_attn",
    )(sinks, qkv, qkv, qkv, qkv, qkv).reshape(batch * seq, A_Q)


def _dilated_attention(qk, v, batch, seq, gi):
    window, dil = DIL_CONFIGS[gi]
    length = seq // dil
    tq = _attn_tile(length)
    sub = tq // ATTN_BLOCK
    qk = qk.reshape(batch, length, dil * 2 * C_QKV)
    v = v.reshape(batch, length, dil * C_QKV)
    n_grp = len(DIL_CONFIGS)
    qk_blocks = 2 * n_grp

    def prev(n):
        return jnp.maximum(n * sub - 1, 0)

    kern = functools.partial(
        _band_attn_kernel, n_pairs=DIL_HEADS_PER_GROUP // 2, kv_block_of_pair=lambda hp: hp,
        max_dist=window // dil, has_sink=False, want_lse=True, tq=tq, seq_axis=2)
    out_sds = jax.ShapeDtypeStruct((batch, length, dil * C_OUT), F32)
    o, l = pl.pallas_call(
        kern,
        grid=(batch, dil, length // tq),
        in_specs=[
            pl.BlockSpec((None, tq, C_OUT), lambda b, r, n: (b, n, r * qk_blocks + gi)),
            pl.BlockSpec((None, tq, C_OUT), lambda b, r, n: (b, n, r * qk_blocks + n_grp + gi)),
            pl.BlockSpec((None, ATTN_BLOCK, C_OUT), lambda b, r, n: (b, prev(n), r * qk_blocks + n_grp + gi)),
            pl.BlockSpec((None, tq, C_OUT), lambda b, r, n: (b, n, r * n_grp + gi)),
            pl.BlockSpec((None, ATTN_BLOCK, C_OUT), lambda b, r, n: (b, prev(n), r * n_grp + gi)),
        ],
        out_specs=[pl.BlockSpec((None, tq, C_OUT), lambda b, r, n: (b, n, r))] * 2,
        out_shape=[out_sds, out_sds],
        scratch_shapes=[pltpu.VMEM((tq + ATTN_BLOCK, C_OUT), BF16), pltpu.VMEM((tq + ATTN_BLOCK, C_OUT), BF16)],
        compiler_params=_cparams("arbitrary", "arbitrary", "arbitrary"),
        name=f"dil_attn_{dil}",
    )(qk, qk, qk, v, v)
    return o.reshape(batch * seq, C_OUT), l.reshape(batch * seq, C_OUT)


def _ssd_kernel(z_ref, xs_ref, bc_ref, dt_ref, cwx_ref, cwbc_ref, cbx_ref, cbbc_ref, dtb_ref, alog_ref,
                dsk_ref, nw_ref, o_ref, extx, extbc, hst):
    q = SSD_CHUNK
    c = pl.program_id(1)

    @pl.when(c == 0)
    def _():
        extx[0:SUBLANES, :] = jnp.zeros((SUBLANES, SSD_INNER), F32)
        extbc[0:SUBLANES, :] = jnp.zeros((SUBLANES, SSD_BC_WIDTH), F32)
        hst[...] = jnp.zeros(hst.shape, F32)

    extx[SUBLANES:, :] = xs_ref[...]
    extbc[SUBLANES:, :] = bc_ref[...]

    def conv_silu(ext, cw_ref, cb_ref):
        acc = cb_ref[...]
        for k in range(SSD_CONV):
            start = SUBLANES - (SSD_CONV - 1) + k
            acc = acc + cw_ref[k:k + 1, :] * ext[start:start + q, :]
        return acc * _sigmoid(acc)

    xa = conv_silu(extx, cwx_ref, cbx_ref)
    bca = conv_silu(extbc, cwbc_ref, cbbc_ref)
    extx[0:SUBLANES, :] = xs_ref[q - SUBLANES:q, :]
    extbc[0:SUBLANES, :] = bc_ref[q - SUBLANES:q, :]

    dtv = dt_ref[...] + dtb_ref[...]
    dt = jnp.maximum(dtv, 0.0) + jnp.log(1.0 + jnp.exp(-jnp.abs(dtv)))
    a = -jnp.exp(alog_ref[...])
    rr = lax.broadcasted_iota(I32, (q, q), 0)
    cc = lax.broadcasted_iota(I32, (q, q), 1)
    causal = rr >= cc
    tril = jnp.where(causal, 1.0, 0.0).astype(F32)
    acs = jnp.dot(tril, dt * a, preferred_element_type=F32, precision=lax.Precision.HIGHEST)
    acs_t = acs.T

    for g in range(SSD_GROUPS):
        bg = bca[:, g * SSD_STATE:(g + 1) * SSD_STATE]
        cg = bca[:, (SSD_GROUPS + g) * SSD_STATE:(SSD_GROUPS + g + 1) * SSD_STATE].astype(BF16)
        bt = bg.T.astype(BF16)
        cb = jnp.dot(cg, bt, preferred_element_type=F32)
        ys = []
        for e in range(SSD_HEADS_PER_GROUP):
            h = g * SSD_HEADS_PER_GROUP + e
            col = acs[:, h:h + 1]
            rowv = acs_t[h:h + 1, :]
            last = acs[q - 1:q, h:h + 1]
            decay = jnp.exp(jnp.where(causal, col - rowv, NEG_INF))
            xh = xa[:, h * SSD_HEADDIM:(h + 1) * SSD_HEADDIM]
            xdt = xh * dt[:, h:h + 1]
            y = jnp.dot((cb * decay).astype(BF16), xdt.astype(BF16), preferred_element_type=F32)
            hprev = hst[h]
            y = y + jnp.dot(cg, hprev.astype(BF16), preferred_element_type=F32) * jnp.exp(col)
            y = y + xh * dsk_ref[:, h * SSD_HEADDIM:(h + 1) * SSD_HEADDIM]
            to_end = jnp.exp(last - col)
            hst[h] = hprev * jnp.exp(last) + jnp.dot(bt, (xdt * to_end).astype(BF16), preferred_element_type=F32)
            ys.append(y)
        lo, hi = g * SSD_GROUP_WIDTH, (g + 1) * SSD_GROUP_WIDTH
        zg = z_ref[:, lo:hi]
        yg = jnp.concatenate(ys, axis=1) * (zg * _sigmoid(zg))
        yg = yg * lax.rsqrt(jnp.mean(yg * yg, axis=-1, keepdims=True) + LN_EPS)
        o_ref[:, lo:hi] = (yg * nw_ref[:, lo:hi]).astype(o_ref.dtype)


def _ssd_mixer(misc, conv_w, conv_b, dt_bias, a_log, d_skip, norm_w, batch, seq):
    q = SSD_CHUNK
    nc = seq // q
    pad = LANES - SSD_HEADS
    row = lambda v: v.reshape(1, -1)
    args = [
        misc, misc, misc, misc,
        conv_w[:, :SSD_INNER], conv_w[:, SSD_INNER:], row(conv_b[:SSD_INNER]), row(conv_b[SSD_INNER:]),
        row(jnp.pad(dt_bias, (0, pad))), row(jnp.pad(a_log, (0, pad))),
        row(jnp.repeat(d_skip, SSD_HEADDIM)), row(norm_w),
    ]
    tok = lambda b, c: b * nc + c
    full = lambda shape: pl.BlockSpec(shape, lambda b, c: (0, 0))
    bc_blk = 2 * SSD_INNER // SSD_BC_WIDTH
    dt_blk = (2 * SSD_INNER + SSD_BC_WIDTH) // LANES
    return pl.pallas_call(
        _ssd_kernel,
        grid=(batch, nc),
        in_specs=[
            pl.BlockSpec((q, SSD_INNER), lambda b, c: (tok(b, c), 0)),
            pl.BlockSpec((q, SSD_INNER), lambda b, c: (tok(b, c), 1)),
            pl.BlockSpec((q, SSD_BC_WIDTH), lambda b, c: (tok(b, c), bc_blk)),
            pl.BlockSpec((q, LANES), lambda b, c: (tok(b, c), dt_blk)),
            full((SSD_CONV, SSD_INNER)), full((SSD_CONV, SSD_BC_WIDTH)), full((1, SSD_INNER)), full((1, SSD_BC_WIDTH)),
            full((1, LANES)), full((1, LANES)), full((1, SSD_INNER)), full((1, SSD_INNER)),
        ],
        out_specs=pl.BlockSpec((q, SSD_INNER), lambda b, c: (tok(b, c), 0)),
        out_shape=jax.ShapeDtypeStruct((batch * seq, SSD_INNER), BF16),
        scratch_shapes=[
            pltpu.VMEM((SUBLANES + q, SSD_INNER), F32),
            pltpu.VMEM((SUBLANES + q, SSD_BC_WIDTH), F32),
            pltpu.VMEM((SSD_HEADS, SSD_STATE, SSD_HEADDIM), F32),
        ],
        compiler_params=_cparams("arbitrary", "arbitrary"),
        name="ssd_mixer",
    )(*args)


def _merge_kernel(ysw_ref, yss_ref, o0_ref, l0_ref, o1_ref, l1_ref, o2_ref, l2_ref, g0_ref, g1_ref, g2_ref,
                  p0_ref, p1_ref, p2_ref, out_ref):
    l0, l1, l2 = l0_ref[...], l1_ref[...], l2_ref[...]
    m = jnp.maximum(jnp.maximum(l0, l1), l2)
    e0, e1, e2 = jnp.exp(l0 - m), jnp.exp(l1 - m), jnp.exp(l2 - m)
    inv = 1.0 / (e0 + e1 + e2)
    ydil = (o0_ref[...] * (e0 * inv) + o1_ref[...] * (e1 * inv) + o2_ref[...] * (e2 * inv)).astype(BF16)
    merged = g0_ref[...] * jnp.dot(ysw_ref[...], p0_ref[...], preferred_element_type=F32)
    merged = merged + g1_ref[...] * jnp.dot(yss_ref[...], p1_ref[...], preferred_element_type=F32)
    merged = merged + g2_ref[...] * jnp.dot(ydil, p2_ref[...], preferred_element_type=F32)
    out_ref[...] = merged.astype(out_ref.dtype)


def _merge(y_swa, y_ssd, dil, gates, p_swa, p_ssd, p_dil, tm, tn):
    t = y_swa.shape[0]
    d = p_swa.shape[1]
    nj = d // tn
    rows = lambda width: pl.BlockSpec((tm, width), lambda i, j: (i, 0))
    gate = lambda k: pl.BlockSpec((tm, tn), lambda i, j: (i, k * nj + j))
    wcol = lambda width: pl.BlockSpec((width, tn), lambda i, j: (0, j))
    dil_args = [a for pair in dil for a in pair]
    return pl.pallas_call(
        _merge_kernel,
        grid=(t // tm, nj),
        in_specs=[rows(A_Q), rows(SSD_INNER)] + [rows(C_OUT)] * 6 + [gate(0), gate(1), gate(2)]
        + [wcol(A_Q), wcol(SSD_INNER), wcol(C_OUT)],
        out_specs=pl.BlockSpec((tm, tn), lambda i, j: (i, j)),
        out_shape=jax.ShapeDtypeStruct((t, d), BF16),
        compiler_params=_cparams("arbitrary", "arbitrary"),
        name="merge",
    )(y_swa, y_ssd, *dil_args, gates, gates, gates, p_swa, p_ssd, p_dil)


def _layer_norm(h, g, b):
    mu = jnp.mean(h, axis=-1, keepdims=True)
    hc = h - mu
    var = jnp.mean(hc * hc, axis=-1, keepdims=True)
    return hc * lax.rsqrt(var + LN_EPS) * g + b


def _ln1_router_kernel(x_ref, m_ref, w_ref, g_ref, b_ref, wr_ref, br_ref, xo_ref, ids_ref, comb_ref, *, alpha):
    h = alpha * x_ref[...] + jnp.dot(m_ref[...], w_ref[...], preferred_element_type=F32)
    y = _layer_norm(h, g_ref[...], b_ref[...])
    xo_ref[...] = y

    logits = jnp.dot(y, wr_ref[...], preferred_element_type=F32, precision=lax.Precision.HIGHEST) + br_ref[...]
    lane = lax.broadcasted_iota(I32, logits.shape, 1).astype(F32)
    big = float(2 * LANES)

    def first_argmax(vals):
        top = jnp.max(vals, axis=-1, keepdims=True)
        return top, jnp.min(jnp.where(vals == top, lane, big), axis=-1, keepdims=True)

    gl = jnp.where(lane < N_GROUPS, logits, NEG_INF)
    gmax, gsel = first_argmax(gl)
    p_group = 1.0 / jnp.sum(jnp.exp(gl - gmax), axis=-1, keepdims=True)
    lo = N_GROUPS + gsel * EXPERTS_PER_GROUP
    el = jnp.where((lane >= lo) & (lane < lo + EXPERTS_PER_GROUP), logits, NEG_INF)
    v1, i1 = first_argmax(el)
    v2, i2 = first_argmax(jnp.where(lane == i1, NEG_INF, el))
    e21 = jnp.exp(v2 - v1)
    inv = 1.0 / (1.0 + e21)
    ids = jnp.where(lane == 0.0, i1 - N_GROUPS, jnp.where(lane == 1.0, i2 - N_GROUPS, 0.0))
    ids_ref[...] = ids.astype(I32)
    comb_ref[...] = jnp.where(lane == 0.0, inv * p_group, jnp.where(lane == 1.0, e21 * inv * p_group, 0.0))


def _ln1_router(x, merged, w_out, g, b, wr, br, alpha, tm):
    t, d = x.shape
    rows = lambda width: pl.BlockSpec((tm, width), lambda i: (i, 0))
    full = lambda shape: pl.BlockSpec(shape, lambda i: (0, 0))
    return pl.pallas_call(
        functools.partial(_ln1_router_kernel, alpha=alpha),
        grid=(t // tm,),
        in_specs=[rows(d), rows(d), full((d, d)), full((1, d)), full((1, d)), full((d, LANES)), full((1, LANES))],
        out_specs=[rows(d), rows(LANES), rows(LANES)],
        out_shape=[jax.ShapeDtypeStruct((t, d), F32),
                   jax.ShapeDtypeStruct((t, LANES), I32), jax.ShapeDtypeStruct((t, LANES), F32)],
        compiler_params=_cparams("arbitrary"),
        name="ln1_router",
    )(x, merged, w_out, g.reshape(1, d), b.reshape(1, d), wr, br)


def _moe_kernel(be_ref, sa_ref, nu_ref, x_hbm, w1_ref, w3_ref, w2_ref, y_hbm, xbuf, ybuf, gsem, ssem, *, n_tok):
    b = pl.program_id(0)
    n_used = nu_ref[0]

    def gather_copy(blk, slot, r):
        tok = jnp.maximum(sa_ref[blk * MOE_BLOCK + r], 0) // TOP_K
        return pltpu.make_async_copy(x_hbm.at[pl.ds(tok, 1), :], xbuf.at[slot, pl.ds(r, 1), :], gsem.at[slot])

    def scatter_copy(blk, r):
        a = jnp.maximum(sa_ref[blk * MOE_BLOCK + r], 0)
        dst = (a % TOP_K) * n_tok + a // TOP_K
        return pltpu.make_async_copy(ybuf.at[pl.ds(r, 1), :], y_hbm.at[pl.ds(dst, 1), :], ssem.at[0])

    def for_rows(fn):
        def body(r, carry):
            fn(r)
            return carry
        lax.fori_loop(0, MOE_BLOCK, body, 0)

    def for_real_rows(blk, fn):
        def body(r, carry):
            pl.when(sa_ref[blk * MOE_BLOCK + r] >= 0)(lambda: fn(r))
            return carry
        lax.fori_loop(0, MOE_BLOCK, body, 0)

    @pl.when(b < n_used)
    def _():
        slot = b % 2

        @pl.when(b == 0)
        def _():
            for_rows(lambda r: gather_copy(0, 0, r).start())

        @pl.when(b + 1 < n_used)
        def _():
            for_rows(lambda r: gather_copy(b + 1, 1 - slot, r).start())

        for_rows(lambda r: gather_copy(b, slot, r).wait())
        xb = xbuf[slot].astype(BF16)
        h1 = jnp.dot(xb, w1_ref[...], preferred_element_type=F32)
        h3 = jnp.dot(xb, w3_ref[...], preferred_element_type=F32)
        h = (h1 * _sigmoid(h1) * h3).astype(BF16)
        y = jnp.dot(h, w2_ref[...], preferred_element_type=F32)

        @pl.when(b > 0)
        def _():
            for_real_rows(b - 1, lambda r: scatter_copy(b - 1, r).wait())

        ybuf[...] = y
        for_real_rows(b, lambda r: scatter_copy(b, r).start())

        @pl.when(b == n_used - 1)
        def _():
            for_real_rows(b, lambda r: scatter_copy(b, r).wait())


def _moe_experts(x1, block_e, slot_assign, n_used, w1, w3, w2):
    t, d = x1.shape
    n_blocks = block_e.shape[0]
    wspec = lambda shape: pl.BlockSpec((None,) + shape, lambda b, be, sa, nu: (be[b], 0, 0))
    grid_spec = pltpu.PrefetchScalarGridSpec(
        num_scalar_prefetch=3,
        grid=(n_blocks,),
        in_specs=[pl.BlockSpec(memory_space=pl.ANY), wspec((d, D_EXPERT)), wspec((d, D_EXPERT)), wspec((D_EXPERT, d))],
        out_specs=pl.BlockSpec(memory_space=pl.ANY),
        scratch_shapes=[
            pltpu.VMEM((2, MOE_BLOCK, d), F32),
            pltpu.VMEM((MOE_BLOCK, d), F32),
            pltpu.SemaphoreType.DMA((2,)),
            pltpu.SemaphoreType.DMA((1,)),
        ],
    )
    return pl.pallas_call(
        functools.partial(_moe_kernel, n_tok=t),
        grid_spec=grid_spec,
        out_shape=jax.ShapeDtypeStruct((TOP_K * t, d), F32),
        compiler_params=_cparams("arbitrary"),
        name="moe_experts",
    )(block_e, slot_assign, n_used, x1, w1, w3, w2)


def _ln2_kernel(x_ref, y0_ref, y1_ref, comb_ref, g_ref, b_ref, xo_ref, xb_ref, *, alpha):
    comb = comb_ref[...]
    h = alpha * x_ref[...] + comb[:, 0:1] * y0_ref[...] + comb[:, 1:2] * y1_ref[...]
    y = _layer_norm(h, g_ref[...], b_ref[...])
    xo_ref[...] = y
    xb_ref[...] = y.astype(BF16)


def _ln2(x1, y2, comb, g, b, alpha, tm):
    t, d = x1.shape
    nblk = t // tm
    rows = lambda width: pl.BlockSpec((tm, width), lambda i: (i, 0))
    full = lambda shape: pl.BlockSpec(shape, lambda i: (0, 0))
    return pl.pallas_call(
        functools.partial(_ln2_kernel, alpha=alpha),
        grid=(nblk,),
        in_specs=[rows(d), rows(d), pl.BlockSpec((tm, d), lambda i: (nblk + i, 0)), rows(LANES), full((1, d)), full((1, d))],
        out_specs=[rows(d), rows(d)],
        out_shape=[jax.ShapeDtypeStruct((t, d), F32), jax.ShapeDtypeStruct((t, d), BF16)],
        compiler_params=_cparams("arbitrary"),
        name="ln2",
    )(x1, y2, y2, comb, g.reshape(1, d), b.reshape(1, d))


def _dispatch_plan(ids, n_tok):
    n_assign = n_tok * TOP_K
    n_blocks = -(-n_assign // MOE_BLOCK) + N_EXPERTS
    flat_e = ids[:, :TOP_K].reshape(n_assign)
    onehot = (flat_e[:, None] == jnp.arange(N_EXPERTS, dtype=I32)[None, :]).astype(I32)
    csum = jnp.cumsum(onehot, axis=0)
    counts = csum[-1]
    rank = jnp.take_along_axis(csum, flat_e[:, None], axis=1)[:, 0] - 1
    padded = (counts + MOE_BLOCK - 1) // MOE_BLOCK * MOE_BLOCK
    pad_end = jnp.cumsum(padded)
    pad_start = pad_end - padded
    dest = pad_start[flat_e] + rank
    slot_assign = jnp.full((n_blocks * MOE_BLOCK,), -1, I32).at[dest].set(jnp.arange(n_assign, dtype=I32))
    block_e = jnp.minimum(
        jnp.searchsorted(pad_end, jnp.arange(n_blocks, dtype=I32) * MOE_BLOCK, side="right"), N_EXPERTS - 1).astype(I32)
    n_used = (pad_end[-1:] // MOE_BLOCK).astype(I32)
    return block_e, slot_assign, n_used


def _rope_tables(seq):
    inv_freq = 1.0 / (ROPE_THETA ** (jnp.arange(0, HEAD_DIM, 2, dtype=F32) / HEAD_DIM))
    ang = jnp.arange(seq, dtype=F32)[:, None] * inv_freq[None, :]
    cos, sin = jnp.cos(ang), jnp.sin(ang)
    reps = LANES // HEAD_DIM
    return jnp.tile(jnp.concatenate([cos, cos], axis=1), (1, reps)), jnp.tile(jnp.concatenate([-sin, sin], axis=1), (1, reps))


def _tile(total, want):
    return min(total, want)


def kernel(x, w_in, b_in, attn_sinks, conv_w, conv_b, dt_bias, a_log, d_skip, ssd_norm_w, proj_swa, proj_ssd, proj_dil, w_out, ln1_g, ln1_b, router_group_w, router_group_b, router_expert_w, router_expert_b, expert_w1, expert_w3, expert_w2, ln2_g, ln2_b):
    batch, seq, d = x.shape
    depth = w_in.shape[0]
    t = batch * seq
    alpha = (2 * depth) ** 0.25
    rope = _rope_tables(seq)
    tm = _tile(seq, 512)

    sizes = (A_Q, A_KV, A_KV, SSD_INNER, SSD_INNER + SSD_BC_WIDTH, SSD_HEADS, C_QKV, C_QKV, C_QKV, 3 * d)
    offs = [0]
    for s in sizes:
        offs.append(offs[-1] + s)
    o_qa, o_ka, o_va, o_z, o_xbc, o_dt, o_qc, o_kc, o_vc, o_gate, o_end = offs

    def dup_heads(m):
        parts = [m[..., h * HEAD_DIM:(h + 1) * HEAD_DIM] for h in range(SWA_KV_HEADS)]
        return jnp.concatenate([p for p in parts for _ in range(2)], axis=-1)

    xf = x.reshape(t, d)
    xb = xf.astype(BF16)
    dt_pad = LANES - SSD_HEADS
    for l in range(depth):
        w, bias = w_in[l], b_in[l]
        w_a = jnp.concatenate([w[:, o_qa:o_ka], dup_heads(w[:, o_ka:o_va]), dup_heads(w[:, o_va:o_z])], axis=1).astype(BF16)
        b_a = jnp.concatenate([bias[o_qa:o_ka], dup_heads(bias[o_ka:o_va]), dup_heads(bias[o_va:o_z])])
        w_misc = jnp.pad(w[:, o_z:o_qc], ((0, 0), (0, dt_pad))).astype(BF16)
        b_misc = jnp.pad(bias[o_z:o_qc], (0, dt_pad))

        qkv_a = _proj(xb, w_a, b_a, BF16, tm, w_a.shape[1], seq, rope=rope, rope_blocks=(A_Q + A_KV_DUP) // LANES, name="proj_swa_qkv")
        qk_c = _proj(xb, w[:, o_qc:o_vc].astype(BF16), bias[o_qc:o_vc], BF16, tm, C_QKV, seq, rope=rope, rope_blocks=C_QKV // LANES, name="proj_dil_qk")
        v_c = _proj(xb, w[:, o_vc:o_gate].astype(BF16), bias[o_vc:o_gate], BF16, tm, C_QKV, seq, name="proj_dil_v")
        misc = _proj(xb, w_misc, b_misc, F32, tm, w_misc.shape[1] // 3, seq, name="proj_ssd")
        gates = _proj(xb, w[:, o_gate:o_end].astype(BF16), bias[o_gate:o_end], F32, tm, 1024, seq, act="sigmoid", name="proj_gates")

        y_swa = _swa_attention(qkv_a, attn_sinks[l], batch, seq)
        y_ssd = _ssd_mixer(misc, conv_w[l], conv_b[l], dt_bias[l], a_log[l], d_skip[l], ssd_norm_w[l], batch, seq)
        dil = [_dilated_attention(qk_c, v_c, batch, seq, gi) for gi in range(len(DIL_CONFIGS))]

        merged = _merge(y_swa, y_ssd, dil, gates, proj_swa[l].astype(BF16), proj_ssd[l].astype(BF16),
                        proj_dil[l].astype(BF16), tm, 1024)

        wr = jnp.concatenate([router_group_w[l], router_expert_w[l].transpose(1, 0, 2).reshape(d, N_EXPERTS)], axis=1)
        wr = jnp.pad(wr, ((0, 0), (0, LANES - wr.shape[1])))
        br = jnp.pad(jnp.concatenate([router_group_b[l], router_expert_b[l].reshape(N_EXPERTS)]), (0, LANES - N_GROUPS - N_EXPERTS))
        x1, ids, comb = _ln1_router(xf, merged, w_out[l].astype(BF16), ln1_g[l], ln1_b[l], wr, br.reshape(1, LANES), alpha, _tile(seq, 256))

        block_e, slot_assign, n_used = _dispatch_plan(ids, t)
        y2 = _moe_experts(x1, block_e, slot_assign, n_used, expert_w1[l].astype(BF16), expert_w3[l].astype(BF16), expert_w2[l].astype(BF16))
        xf, xb = _ln2(x1, y2, comb, ln2_g[l], ln2_b[l], alpha, tm)
    return xf.reshape(batch, seq, d)
```

```python
import functools

import jax
import jax.numpy as jnp
from jax import lax
from jax.experimental import pallas as pl
from jax.experimental.pallas import tpu as pltpu

F32 = jnp.float32
BF16 = jnp.bfloat16
I32 = jnp.int32

LANES = 128
SUBLANES = 8
VMEM_LIMIT_BYTES = 48 * 1024 * 1024

HEAD_DIM = 64
HALF_HEAD = HEAD_DIM // 2
ROPE_THETA = 10000.0
ATTN_BLOCK = 128
LN_EPS = 1e-5

SWA_Q_HEADS = 16
SWA_KV_HEADS = 2
SWA_WINDOW = 128

SSD_HEADS = 24
SSD_HEADDIM = 64
SSD_INNER = SSD_HEADS * SSD_HEADDIM
SSD_GROUPS = 4
SSD_STATE = 128
SSD_CONV = 4
SSD_CHUNK = 128
SSD_HEADS_PER_GROUP = SSD_HEADS // SSD_GROUPS
SSD_GROUP_WIDTH = SSD_INNER // SSD_GROUPS
SSD_BC_WIDTH = 2 * SSD_GROUPS * SSD_STATE

DIL_CONFIGS = ((128, 1), (512, 4), (2048, 16))
DIL_HEADS_PER_GROUP = 6
DIL_HEADS = DIL_HEADS_PER_GROUP * len(DIL_CONFIGS)

A_Q = SWA_Q_HEADS * HEAD_DIM
A_KV = SWA_KV_HEADS * HEAD_DIM
A_KV_DUP = 2 * A_KV
C_QKV = DIL_HEADS * HEAD_DIM
C_OUT = DIL_HEADS_PER_GROUP * HEAD_DIM

N_GROUPS = 4
EXPERTS_PER_GROUP = 8
N_EXPERTS = N_GROUPS * EXPERTS_PER_GROUP
TOP_K = 2
D_EXPERT = 512
MOE_BLOCK = 128
MOE_DMA_UNROLL = 8

NEG_INF = float("-inf")


def _cparams(*sem):
    return pltpu.CompilerParams(dimension_semantics=sem, vmem_limit_bytes=VMEM_LIMIT_BYTES)


def _sigmoid(v):
    return 1.0 / (1.0 + jnp.exp(-v))


def _proj_kernel(*refs, rope_blocks, act):
    if rope_blocks:
        x_ref, w_ref, b_ref, cos_ref, sin_ref, o_ref = refs
    else:
        x_ref, w_ref, b_ref, o_ref = refs
    acc = jnp.dot(x_ref[...], w_ref[...], preferred_element_type=F32) + b_ref[...]
    if rope_blocks:
        cos = cos_ref[...]
        sin = sin_ref[...]
        lane = lax.broadcasted_iota(I32, cos.shape, 1)
        first_half = (lane % HEAD_DIM) < HALF_HEAD
        for c in range(acc.shape[1] // LANES):
            v = acc[:, c * LANES:(c + 1) * LANES]
            if c < rope_blocks:
                partner = jnp.where(first_half, pltpu.roll(v, LANES - HALF_HEAD, 1), pltpu.roll(v, HALF_HEAD, 1))
                v = v * cos + partner * sin
            o_ref[:, c * LANES:(c + 1) * LANES] = v.astype(o_ref.dtype)
    elif act == "sigmoid":
        o_ref[...] = _sigmoid(acc).astype(o_ref.dtype)
    else:
        o_ref[...] = acc.astype(o_ref.dtype)


def _proj(xb, w, b, out_dtype, tm, tn, seq, rope=None, rope_blocks=0, act=None, name="proj"):
    t, d = xb.shape
    n = w.shape[1]
    assert t % tm == 0 and n % tn == 0 and seq % tm == 0
    in_specs = [
        pl.BlockSpec((tm, d), lambda j, i: (i, 0)),
        pl.BlockSpec((d, tn), lambda j, i: (0, j)),
        pl.BlockSpec((1, tn), lambda j, i: (0, j)),
    ]
    args = [xb, w, b.reshape(1, n)]
    if rope_blocks:
        sblocks = seq // tm
        in_specs += [pl.BlockSpec((tm, LANES), lambda j, i: (i % sblocks, 0))] * 2
        args += list(rope)
    return pl.pallas_call(
        functools.partial(_proj_kernel, rope_blocks=rope_blocks, act=act),
        grid=(n // tn, t // tm),
        in_specs=in_specs,
        out_specs=pl.BlockSpec((tm, tn), lambda j, i: (i, j)),
        out_shape=jax.ShapeDtypeStruct((t, n), out_dtype),
        compiler_params=_cparams("arbitrary", "arbitrary"),
        name=name,
    )(*args)


def _band_biases(max_dist):
    row = lax.broadcasted_iota(I32, (ATTN_BLOCK, 2 * ATTN_BLOCK), 0)
    col = lax.broadcasted_iota(I32, (ATTN_BLOCK, 2 * ATTN_BLOCK), 1)
    dist = row + ATTN_BLOCK - col
    in_band = (dist >= 0) & (dist <= max_dist)
    bias_any = jnp.where(in_band, 0.0, NEG_INF).astype(F32)
    bias_first = jnp.where(in_band & (col >= ATTN_BLOCK), 0.0, NEG_INF).astype(F32)
    return bias_any, bias_first


def _attend_pair(q2, k2, v2, bias, sinks):
    lane = lax.broadcasted_iota(I32, (ATTN_BLOCK, LANES), 1)
    low = lane < HEAD_DIM
    scale = HEAD_DIM ** -0.5
    outs, lses = [], []
    for half in range(2):
        qh = jnp.where(low if half == 0 else jnp.logical_not(low), q2, jnp.zeros_like(q2))
        s = lax.dot_general(qh, k2, (((1,), (1,)), ((), ())), preferred_element_type=F32) * scale
        s = s + bias
        m = jnp.max(s, axis=-1, keepdims=True)
        if sinks is not None:
            m = jnp.maximum(m, sinks[half])
        p = jnp.exp(s - m)
        den = jnp.sum(p, axis=-1, keepdims=True)
        if sinks is not None:
            den = den + jnp.exp(sinks[half] - m)
        outs.append(jnp.dot(p.astype(BF16), v2, preferred_element_type=F32) / den)
        lses.append(m + jnp.log(den))
    return jnp.where(low, outs[0], outs[1]), jnp.where(low, lses[0], lses[1])


def _swa_attn_kernel(sink_ref, q_ref, kc_ref, kp_ref, vc_ref, vp_ref, o_ref, kf, vf, *, pairs_per_kv, tq):
    n = pl.program_id(1)
    kf[0:ATTN_BLOCK, :] = kp_ref[...]
    kf[ATTN_BLOCK:, :] = kc_ref[...]
    vf[0:ATTN_BLOCK, :] = vp_ref[...]
    vf[ATTN_BLOCK:, :] = vc_ref[...]
    bias_any, bias_first = _band_biases(SWA_WINDOW - 1)

    def sub_block(j, carry):
        r0 = pl.multiple_of(j * ATTN_BLOCK, ATTN_BLOCK)
        bias = jnp.where(jnp.logical_and(n == 0, j == 0), bias_first, bias_any)
        for hp in range(SWA_Q_HEADS // 2):
            cols = slice(hp * LANES, (hp + 1) * LANES)
            kv_cols = slice((hp // pairs_per_kv) * LANES, (hp // pairs_per_kv + 1) * LANES)
            o, _ = _attend_pair(q_ref[pl.ds(r0, ATTN_BLOCK), cols], kf[pl.ds(r0, 2 * ATTN_BLOCK), kv_cols],
                                vf[pl.ds(r0, 2 * ATTN_BLOCK), kv_cols], bias, (sink_ref[2 * hp], sink_ref[2 * hp + 1]))
            o_ref[pl.ds(r0, ATTN_BLOCK), cols] = o.astype(o_ref.dtype)
        return carry

    lax.fori_loop(0, tq // ATTN_BLOCK, sub_block, 0)


def _dil_attn_kernel(q_ref, kc_ref, kp_ref, vc_ref, vp_ref, o_ref, l_ref, *, dil, max_dist):
    n = pl.program_id(1)
    rows = q_ref.shape[0]
    sub_rows = rows // dil
    bias_any, bias_first = _band_biases(max_dist)
    bias_head = jnp.where(n == 0, bias_first, bias_any)

    def rows_of(ref, r, start, count):
        if dil == 1:
            return ref[start:start + count, :]
        return ref[pl.ds(start * dil + r, count, stride=dil), :]

    for r in range(dil):
        k_prev = rows_of(kp_ref, r, 0, ATTN_BLOCK).astype(BF16)
        v_prev = rows_of(vp_ref, r, 0, ATTN_BLOCK).astype(BF16)
        for j in range(sub_rows // ATTN_BLOCK):
            start = j * ATTN_BLOCK
            k_cur = rows_of(kc_ref, r, start, ATTN_BLOCK).astype(BF16)
            v_cur = rows_of(vc_ref, r, start, ATTN_BLOCK).astype(BF16)
            q2 = rows_of(q_ref, r, start, ATTN_BLOCK).astype(BF16)
            o, l = _attend_pair(q2, jnp.concatenate([k_prev, k_cur], axis=0), jnp.concatenate([v_prev, v_cur], axis=0),
                                bias_head if j == 0 else bias_any, None)
            if dil == 1:
                o_ref[start:start + ATTN_BLOCK, :] = o
                l_ref[start:start + ATTN_BLOCK, :] = l
            else:
                o_ref[pl.ds(start * dil + r, ATTN_BLOCK, stride=dil), :] = o
                l_ref[pl.ds(start * dil + r, ATTN_BLOCK, stride=dil), :] = l
            k_prev, v_prev = k_cur, v_cur


def _attn_tile(length):
    return min(512, length)


def _swa_attention(qkv, sinks, batch, seq):
    tq = _attn_tile(seq)
    qkv = qkv.reshape(batch, seq, A_Q + 2 * A_KV_DUP)
    k_blk = A_Q // A_KV_DUP
    v_blk = k_blk + 1
    sub = tq // ATTN_BLOCK

    def prev(n):
        return jnp.maximum(n * sub - 1, 0)

    kern = functools.partial(_swa_attn_kernel, pairs_per_kv=SWA_Q_HEADS // SWA_KV_HEADS // 2, tq=tq)
    return pl.pallas_call(
        kern,
        grid=(batch, seq // tq),
        in_specs=[
            pl.BlockSpec(memory_space=pltpu.SMEM),
            pl.BlockSpec((None, tq, A_Q), lambda b, n: (b, n, 0)),
            pl.BlockSpec((None, tq, A_KV_DUP), lambda b, n: (b, n, k_blk)),
            pl.BlockSpec((None, ATTN_BLOCK, A_KV_DUP), lambda b, n: (b, prev(n), k_blk)),
            pl.BlockSpec((None, tq, A_KV_DUP), lambda b, n: (b, n, v_blk)),
            pl.BlockSpec((None, ATTN_BLOCK, A_KV_DUP), lambda b, n: (b, prev(n), v_blk)),
        ],
        out_specs=pl.BlockSpec((None, tq, A_Q), lambda b, n: (b, n, 0)),
        out_shape=jax.ShapeDtypeStruct((batch, seq, A_Q), BF16),
        scratch_shapes=[pltpu.VMEM((tq + ATTN_BLOCK, A_KV_DUP), BF16), pltpu.VMEM((tq + ATTN_BLOCK, A_KV_DUP), BF16)],
        compiler_params=_cparams("arbitrary", "arbitrary"),
        name="swa_attn",
    )(sinks, qkv, qkv, qkv, qkv, qkv).reshape(batch * seq, A_Q)


def _dilated_attention(qk, v, batch, seq, gi):
    window, dil = DIL_CONFIGS[gi]
    back = ATTN_BLOCK * dil
    rows = max(_attn_tile(seq), back)
    assert seq % rows == 0 and rows % back == 0
    pairs = DIL_HEADS_PER_GROUP // 2
    q_blk, k_blk, v_blk = gi * pairs, (C_QKV // LANES) + gi * pairs, gi * pairs
    per_seq = seq // rows
    ratio = rows // back

    def cur(col0):
        return pl.BlockSpec((rows, LANES), lambda b, n, hp: (b * per_seq + n, col0 + hp))

    def prev(col0):
        return pl.BlockSpec((back, LANES), lambda b, n, hp: (b * per_seq * ratio + jnp.maximum(n * ratio - 1, 0), col0 + hp))

    out_sds = jax.ShapeDtypeStruct((batch * seq, C_OUT), F32)
    out_spec = pl.BlockSpec((rows, LANES), lambda b, n, hp: (b * per_seq + n, hp))
    return pl.pallas_call(
        functools.partial(_dil_attn_kernel, dil=dil, max_dist=window // dil),
        grid=(batch, per_seq, pairs),
        in_specs=[cur(q_blk), cur(k_blk), prev(k_blk), cur(v_blk), prev(v_blk)],
        out_specs=[out_spec, out_spec],
        out_shape=[out_sds, out_sds],
        compiler_params=_cparams("arbitrary", "arbitrary", "arbitrary"),
        name=f"dil_attn_{dil}",
    )(qk, qk, qk, v, v)


def _ssd_kernel(z_ref, xs_ref, bc_ref, dt_ref, cwx_ref, cwbc_ref, cbx_ref, cbbc_ref, dtb_ref, alog_ref,
                dsk_ref, nw_ref, o_ref, extx, extbc, hst):
    q = SSD_CHUNK
    c = pl.program_id(1)

    @pl.when(c == 0)
    def _():
        extx[0:SUBLANES, :] = jnp.zeros((SUBLANES, SSD_INNER), F32)
        extbc[0:SUBLANES, :] = jnp.zeros((SUBLANES, SSD_BC_WIDTH), F32)
        hst[...] = jnp.zeros(hst.shape, F32)

    extx[SUBLANES:, :] = xs_ref[...]
    extbc[SUBLANES:, :] = bc_ref[...]

    def conv_silu(ext, cw_ref, cb_ref):
        acc = cb_ref[...]
        for k in range(SSD_CONV):
            start = SUBLANES - (SSD_CONV - 1) + k
            acc = acc + cw_ref[k:k + 1, :] * ext[start:start + q, :]
        return acc * _sigmoid(acc)

    xa = conv_silu(extx, cwx_ref, cbx_ref)
    bca = conv_silu(extbc, cwbc_ref, cbbc_ref)
    extx[0:SUBLANES, :] = xs_ref[q - SUBLANES:q, :]
    extbc[0:SUBLANES, :] = bc_ref[q - SUBLANES:q, :]

    dtv = dt_ref[...] + dtb_ref[...]
    dt = jnp.maximum(dtv, 0.0) + jnp.log(1.0 + jnp.exp(-jnp.abs(dtv)))
    a = -jnp.exp(alog_ref[...])
    rr = lax.broadcasted_iota(I32, (q, q), 0)
    cc = lax.broadcasted_iota(I32, (q, q), 1)
    causal = rr >= cc
    tril = jnp.where(causal, 1.0, 0.0).astype(F32)
    acs = jnp.dot(tril, dt * a, preferred_element_type=F32, precision=lax.Precision.HIGHEST)
    acs_t = acs.T

    for g in range(SSD_GROUPS):
        bg = bca[:, g * SSD_STATE:(g + 1) * SSD_STATE]
        cg = bca[:, (SSD_GROUPS + g) * SSD_STATE:(SSD_GROUPS + g + 1) * SSD_STATE].astype(BF16)
        bt = bg.T.astype(BF16)
        cb = jnp.dot(cg, bt, preferred_element_type=F32)
        ys = []
        for e in range(SSD_HEADS_PER_GROUP):
            h = g * SSD_HEADS_PER_GROUP + e
            col = acs[:, h:h + 1]
            rowv = acs_t[h:h + 1, :]
            last = acs[q - 1:q, h:h + 1]
            decay = jnp.exp(jnp.where(causal, col - rowv, NEG_INF))
            xh = xa[:, h * SSD_HEADDIM:(h + 1) * SSD_HEADDIM]
            xdt = xh * dt[:, h:h + 1]
            y = jnp.dot((cb * decay).astype(BF16), xdt.astype(BF16), preferred_element_type=F32)
            hprev = hst[h]
            y = y + jnp.dot(cg, hprev.astype(BF16), preferred_element_type=F32) * jnp.exp(col)
            y = y + xh * dsk_ref[:, h * SSD_HEADDIM:(h + 1) * SSD_HEADDIM]
            to_end = jnp.exp(last - col)
            hst[h] = hprev * jnp.exp(last) + jnp.dot(bt, (xdt * to_end).astype(BF16), preferred_element_type=F32)
            ys.append(y)
        lo, hi = g * SSD_GROUP_WIDTH, (g + 1) * SSD_GROUP_WIDTH
        zg = z_ref[:, lo:hi]
        yg = jnp.concatenate(ys, axis=1) * (zg * _sigmoid(zg))
        yg = yg * lax.rsqrt(jnp.mean(yg * yg, axis=-1, keepdims=True) + LN_EPS)
        o_ref[:, lo:hi] = (yg * nw_ref[:, lo:hi]).astype(o_ref.dtype)


def _ssd_mixer(misc, conv_w, conv_b, dt_bias, a_log, d_skip, norm_w, batch, seq):
    q = SSD_CHUNK
    nc = seq // q
    pad = LANES - SSD_HEADS
    row = lambda v: v.reshape(1, -1)
    args = [
        misc, misc, misc, misc,
        conv_w[:, :SSD_INNER], conv_w[:, SSD_INNER:], row(conv_b[:SSD_INNER]), row(conv_b[SSD_INNER:]),
        row(jnp.pad(dt_bias, (0, pad))), row(jnp.pad(a_log, (0, pad))),
        row(jnp.repeat(d_skip, SSD_HEADDIM)), row(norm_w),
    ]
    tok = lambda b, c: b * nc + c
    full = lambda shape: pl.BlockSpec(shape, lambda b, c: (0, 0))
    bc_blk = 2 * SSD_INNER // SSD_BC_WIDTH
    dt_blk = (2 * SSD_INNER + SSD_BC_WIDTH) // LANES
    return pl.pallas_call(
        _ssd_kernel,
        grid=(batch, nc),
        in_specs=[
            pl.BlockSpec((q, SSD_INNER), lambda b, c: (tok(b, c), 0)),
            pl.BlockSpec((q, SSD_INNER), lambda b, c: (tok(b, c), 1)),
            pl.BlockSpec((q, SSD_BC_WIDTH), lambda b, c: (tok(b, c), bc_blk)),
            pl.BlockSpec((q, LANES), lambda b, c: (tok(b, c), dt_blk)),
            full((SSD_CONV, SSD_INNER)), full((SSD_CONV, SSD_BC_WIDTH)), full((1, SSD_INNER)), full((1, SSD_BC_WIDTH)),
            full((1, LANES)), full((1, LANES)), full((1, SSD_INNER)), full((1, SSD_INNER)),
        ],
        out_specs=pl.BlockSpec((q, SSD_INNER), lambda b, c: (tok(b, c), 0)),
        out_shape=jax.ShapeDtypeStruct((batch * seq, SSD_INNER), BF16),
        scratch_shapes=[
            pltpu.VMEM((SUBLANES + q, SSD_INNER), F32),
            pltpu.VMEM((SUBLANES + q, SSD_BC_WIDTH), F32),
            pltpu.VMEM((SSD_HEADS, SSD_STATE, SSD_HEADDIM), F32),
        ],
        compiler_params=_cparams("arbitrary", "arbitrary"),
        name="ssd_mixer",
    )(*args)


def _merge_kernel(ysw_ref, yss_ref, o0_ref, l0_ref, o1_ref, l1_ref, o2_ref, l2_ref, g0_ref, g1_ref, g2_ref,
                  p0_ref, p1_ref, p2_ref, out_ref):
    l0, l1, l2 = l0_ref[...], l1_ref[...], l2_ref[...]
    m = jnp.maximum(jnp.maximum(l0, l1), l2)
    e0, e1, e2 = jnp.exp(l0 - m), jnp.exp(l1 - m), jnp.exp(l2 - m)
    inv = 1.0 / (e0 + e1 + e2)
    ydil = (o0_ref[...] * (e0 * inv) + o1_ref[...] * (e1 * inv) + o2_ref[...] * (e2 * inv)).astype(BF16)
    merged = g0_ref[...] * jnp.dot(ysw_ref[...], p0_ref[...], preferred_element_type=F32)
    merged = merged + g1_ref[...] * jnp.dot(yss_ref[...], p1_ref[...], preferred_element_type=F32)
    merged = merged + g2_ref[...] * jnp.dot(ydil, p2_ref[...], preferred_element_type=F32)
    out_ref[...] = merged.astype(out_ref.dtype)


def _merge(y_swa, y_ssd, dil, gates, p_swa, p_ssd, p_dil, tm, tn):
    t = y_swa.shape[0]
    d = p_swa.shape[1]
    nj = d // tn
    rows = lambda width: pl.BlockSpec((tm, width), lambda i, j: (i, 0))
    gate = lambda k: pl.BlockSpec((tm, tn), lambda i, j: (i, k * nj + j))
    wcol = lambda width: pl.BlockSpec((width, tn), lambda i, j: (0, j))
    dil_args = [a for pair in dil for a in pair]
    return pl.pallas_call(
        _merge_kernel,
        grid=(t // tm, nj),
        in_specs=[rows(A_Q), rows(SSD_INNER)] + [rows(C_OUT)] * 6 + [gate(0), gate(1), gate(2)]
        + [wcol(A_Q), wcol(SSD_INNER), wcol(C_OUT)],
        out_specs=pl.BlockSpec((tm, tn), lambda i, j: (i, j)),
        out_shape=jax.ShapeDtypeStruct((t, d), BF16),
        compiler_params=_cparams("arbitrary", "arbitrary"),
        name="merge",
    )(y_swa, y_ssd, *dil_args, gates, gates, gates, p_swa, p_ssd, p_dil)


def _layer_norm(h, g, b):
    mu = jnp.mean(h, axis=-1, keepdims=True)
    hc = h - mu
    var = jnp.mean(hc * hc, axis=-1, keepdims=True)
    return hc * lax.rsqrt(var + LN_EPS) * g + b


def _ln1_router_kernel(x_ref, m_ref, w_ref, g_ref, b_ref, wr_ref, br_ref, xo_ref, ids_ref, comb_ref, cnt_ref, *, alpha):
    i = pl.program_id(0)
    h = alpha * x_ref[...] + jnp.dot(m_ref[...], w_ref[...], preferred_element_type=F32)
    y = _layer_norm(h, g_ref[...], b_ref[...])
    xo_ref[...] = y

    logits = jnp.dot(y.astype(BF16), wr_ref[...], preferred_element_type=F32) + br_ref[...]
    lane = lax.broadcasted_iota(I32, logits.shape, 1).astype(F32)
    big = float(2 * LANES)

    def first_argmax(vals):
        top = jnp.max(vals, axis=-1, keepdims=True)
        return top, jnp.min(jnp.where(vals == top, lane, big), axis=-1, keepdims=True)

    gl = jnp.where(lane < N_GROUPS, logits, NEG_INF)
    gmax, gsel = first_argmax(gl)
    p_group = 1.0 / jnp.sum(jnp.exp(gl - gmax), axis=-1, keepdims=True)
    lo = N_GROUPS + gsel * EXPERTS_PER_GROUP
    el = jnp.where((lane >= lo) & (lane < lo + EXPERTS_PER_GROUP), logits, NEG_INF)
    v1, i1 = first_argmax(el)
    v2, i2 = first_argmax(jnp.where(lane == i1, NEG_INF, el))
    e21 = jnp.exp(v2 - v1)
    inv = 1.0 / (1.0 + e21)
    comb_ref[...] = jnp.where(lane == 0.0, inv * p_group, jnp.where(lane == 1.0, e21 * inv * p_group, 0.0))

    @pl.when(i == 0)
    def _():
        cnt_ref[...] = jnp.zeros(cnt_ref.shape, F32)

    pick1 = jnp.where(lane == i1, 1.0, 0.0)
    pick2 = jnp.where(lane == i2, 1.0, 0.0)
    picks = pick1 + pick2
    tm = picks.shape[0]
    earlier = lax.broadcasted_iota(I32, (tm, tm), 0) > lax.broadcasted_iota(I32, (tm, tm), 1)
    before = jnp.dot(jnp.where(earlier, 1.0, 0.0).astype(BF16), picks.astype(BF16), preferred_element_type=F32) + cnt_ref[...]
    rank1 = jnp.sum(pick1 * before, axis=-1, keepdims=True)
    rank2 = jnp.sum(pick2 * before, axis=-1, keepdims=True)
    cnt_ref[...] = cnt_ref[...] + jnp.sum(picks, axis=0, keepdims=True)
    ids = jnp.where(lane == 0.0, i1 - N_GROUPS, jnp.where(lane == 1.0, i2 - N_GROUPS,
                    jnp.where(lane == 2.0, rank1, jnp.where(lane == 3.0, rank2, 0.0))))
    ids_ref[...] = ids.astype(I32)


def _ln1_router(x, merged, w_out, g, b, wr, br, alpha, tm):
    t, d = x.shape
    rows = lambda width: pl.BlockSpec((tm, width), lambda i: (i, 0))
    full = lambda shape: pl.BlockSpec(shape, lambda i: (0, 0))
    once = lambda shape: pl.BlockSpec(shape, lambda i: (0, 0), pipeline_mode=pl.Buffered(1))
    return pl.pallas_call(
        functools.partial(_ln1_router_kernel, alpha=alpha),
        grid=(t // tm,),
        in_specs=[rows(d), rows(d), once((d, d)), full((1, d)), full((1, d)), once((d, LANES)), full((1, LANES))],
        out_specs=[rows(d), rows(LANES), rows(LANES), full((1, LANES))],
        out_shape=[jax.ShapeDtypeStruct((t, d), F32),
                   jax.ShapeDtypeStruct((t, LANES), I32), jax.ShapeDtypeStruct((t, LANES), F32),
                   jax.ShapeDtypeStruct((1, LANES), F32)],
        compiler_params=_cparams("arbitrary"),
        name="ln1_router",
    )(x, merged, w_out, g.reshape(1, d), b.reshape(1, d), wr, br)


def _moe_kernel(be_ref, tok_ref, dst_ref, cnt_ref, nu_ref, x_hbm, w1_ref, w3_ref, w2_ref, y_hbm, xbuf, ybuf, gsem, ssem):
    b = pl.program_id(0)
    n_used = nu_ref[0]

    def gather_copy(blk, slot, r):
        return pltpu.make_async_copy(
            x_hbm.at[pl.ds(tok_ref[blk * MOE_BLOCK + r], 1), :], xbuf.at[slot, pl.ds(r, 1), :], gsem.at[slot])

    def scatter_copy(blk, r):
        return pltpu.make_async_copy(
            ybuf.at[pl.ds(r, 1), :], y_hbm.at[pl.ds(dst_ref[blk * MOE_BLOCK + r], 1), :], ssem.at[0])

    def for_rows(count, fn):
        def chunk(c, carry):
            for u in range(MOE_DMA_UNROLL):
                fn(c * MOE_DMA_UNROLL + u)
            return carry

        def single(r, carry):
            fn(r)
            return carry

        if isinstance(count, int):
            lax.fori_loop(0, count // MOE_DMA_UNROLL, chunk, 0)
        else:
            whole = count // MOE_DMA_UNROLL
            lax.fori_loop(0, whole, chunk, 0)
            lax.fori_loop(whole * MOE_DMA_UNROLL, count, single, 0)

    @pl.when(b < n_used)
    def _():
        slot = b % 2

        @pl.when(b == 0)
        def _():
            for_rows(MOE_BLOCK, lambda r: gather_copy(0, 0, r).start())

        @pl.when(b + 1 < n_used)
        def _():
            for_rows(MOE_BLOCK, lambda r: gather_copy(b + 1, 1 - slot, r).start())

        for_rows(MOE_BLOCK, lambda r: gather_copy(b, slot, r).wait())
        xb = xbuf[slot].astype(BF16)
        h1 = jnp.dot(xb, w1_ref[...], preferred_element_type=F32)
        h3 = jnp.dot(xb, w3_ref[...], preferred_element_type=F32)
        h = (h1 * _sigmoid(h1) * h3).astype(BF16)
        y = jnp.dot(h, w2_ref[...], preferred_element_type=F32)

        @pl.when(b > 0)
        def _():
            for_rows(cnt_ref[b - 1], lambda r: scatter_copy(b - 1, r).wait())

        ybuf[...] = y
        for_rows(cnt_ref[b], lambda r: scatter_copy(b, r).start())

        @pl.when(b == n_used - 1)
        def _():
            for_rows(cnt_ref[b], lambda r: scatter_copy(b, r).wait())


def _moe_experts(x1, plan, w1, w3, w2):
    t, d = x1.shape
    block_e, slot_tok, slot_dst, block_cnt, n_used = plan
    n_blocks = block_e.shape[0]
    wspec = lambda shape: pl.BlockSpec((None,) + shape, lambda b, be, *_: (be[b], 0, 0))
    grid_spec = pltpu.PrefetchScalarGridSpec(
        num_scalar_prefetch=5,
        grid=(n_blocks,),
        in_specs=[pl.BlockSpec(memory_space=pl.ANY), wspec((d, D_EXPERT)), wspec((d, D_EXPERT)), wspec((D_EXPERT, d))],
        out_specs=pl.BlockSpec(memory_space=pl.ANY),
        scratch_shapes=[
            pltpu.VMEM((2, MOE_BLOCK, d), F32),
            pltpu.VMEM((MOE_BLOCK, d), F32),
            pltpu.SemaphoreType.DMA((2,)),
            pltpu.SemaphoreType.DMA((1,)),
        ],
    )
    return pl.pallas_call(
        _moe_kernel,
        grid_spec=grid_spec,
        out_shape=jax.ShapeDtypeStruct((TOP_K * t, d), F32),
        compiler_params=_cparams("arbitrary"),
        name="moe_experts",
    )(block_e, slot_tok, slot_dst, block_cnt, n_used, x1, w1, w3, w2)


def _ln2_kernel(x_ref, y0_ref, y1_ref, comb_ref, g_ref, b_ref, xo_ref, xb_ref, *, alpha):
    comb = comb_ref[...]
    h = alpha * x_ref[...] + comb[:, 0:1] * y0_ref[...] + comb[:, 1:2] * y1_ref[...]
    y = _layer_norm(h, g_ref[...], b_ref[...])
    xo_ref[...] = y
    xb_ref[...] = y.astype(BF16)


def _ln2(x1, y2, comb, g, b, alpha, tm):
    t, d = x1.shape
    nblk = t // tm
    rows = lambda width: pl.BlockSpec((tm, width), lambda i: (i, 0))
    full = lambda shape: pl.BlockSpec(shape, lambda i: (0, 0))
    return pl.pallas_call(
        functools.partial(_ln2_kernel, alpha=alpha),
        grid=(nblk,),
        in_specs=[rows(d), rows(d), pl.BlockSpec((tm, d), lambda i: (nblk + i, 0)), rows(LANES), full((1, d)), full((1, d))],
        out_specs=[rows(d), rows(d)],
        out_shape=[jax.ShapeDtypeStruct((t, d), F32), jax.ShapeDtypeStruct((t, d), BF16)],
        compiler_params=_cparams("arbitrary"),
        name="ln2",
    )(x1, y2, y2, comb, g.reshape(1, d), b.reshape(1, d))


def _dispatch_plan(ids, counts, n_tok):
    n_assign = n_tok * TOP_K
    n_blocks = -(-n_assign // MOE_BLOCK) + N_EXPERTS
    flat_e = ids[:, :TOP_K].reshape(n_assign)
    rank = ids[:, TOP_K:2 * TOP_K].reshape(n_assign)
    padded = (counts + MOE_BLOCK - 1) // MOE_BLOCK * MOE_BLOCK
    pad_end = jnp.cumsum(padded)
    pad_start = pad_end - padded
    onehot = flat_e[:, None] == jnp.arange(N_EXPERTS, dtype=I32)[None, :]
    dest = jnp.sum(jnp.where(onehot, pad_start[None, :], 0), axis=1) + rank
    assign = jnp.arange(n_assign, dtype=I32)
    slot_assign = jnp.full((n_blocks * MOE_BLOCK,), -1, I32).at[dest].set(assign, unique_indices=True)
    real = slot_assign >= 0
    sa = jnp.maximum(slot_assign, 0)
    slot_tok = sa // TOP_K
    slot_dst = (sa % TOP_K) * n_tok + sa // TOP_K
    block_cnt = jnp.sum(real.reshape(n_blocks, MOE_BLOCK), axis=1).astype(I32)
    block_e = jnp.minimum(
        jnp.searchsorted(pad_end, jnp.arange(n_blocks, dtype=I32) * MOE_BLOCK, side="right"), N_EXPERTS - 1).astype(I32)
    n_used = (pad_end[-1:] // MOE_BLOCK).astype(I32)
    return block_e, slot_tok, slot_dst, block_cnt, n_used


def _rope_tables(seq):
    inv_freq = 1.0 / (ROPE_THETA ** (jnp.arange(0, HEAD_DIM, 2, dtype=F32) / HEAD_DIM))
    ang = jnp.arange(seq, dtype=F32)[:, None] * inv_freq[None, :]
    cos, sin = jnp.cos(ang), jnp.sin(ang)
    reps = LANES // HEAD_DIM
    return jnp.tile(jnp.concatenate([cos, cos], axis=1), (1, reps)), jnp.tile(jnp.concatenate([-sin, sin], axis=1), (1, reps))


def _tile(total, want):
    return min(total, want)


def kernel(x, w_in, b_in, attn_sinks, conv_w, conv_b, dt_bias, a_log, d_skip, ssd_norm_w, proj_swa, proj_ssd, proj_dil, w_out, ln1_g, ln1_b, router_group_w, router_group_b, router_expert_w, router_expert_b, expert_w1, expert_w3, expert_w2, ln2_g, ln2_b):
    batch, seq, d = x.shape
    depth = w_in.shape[0]
    t = batch * seq
    alpha = (2 * depth) ** 0.25
    rope = _rope_tables(seq)
    tm = _tile(seq, 512)

    sizes = (A_Q, A_KV, A_KV, SSD_INNER, SSD_INNER + SSD_BC_WIDTH, SSD_HEADS, C_QKV, C_QKV, C_QKV, 3 * d)
    offs = [0]
    for s in sizes:
        offs.append(offs[-1] + s)
    o_qa, o_ka, o_va, o_z, o_xbc, o_dt, o_qc, o_kc, o_vc, o_gate, o_end = offs

    def dup_heads(m):
        parts = [m[..., h * HEAD_DIM:(h + 1) * HEAD_DIM] for h in range(SWA_KV_HEADS)]
        return jnp.concatenate([p for p in parts for _ in range(2)], axis=-1)

    xf = x.reshape(t, d)
    xb = xf.astype(BF16)
    dt_pad = LANES - SSD_HEADS
    for l in range(depth):
        w, bias = w_in[l], b_in[l]
        w_a = jnp.concatenate([w[:, o_qa:o_ka], dup_heads(w[:, o_ka:o_va]), dup_heads(w[:, o_va:o_z])], axis=1).astype(BF16)
        b_a = jnp.concatenate([bias[o_qa:o_ka], dup_heads(bias[o_ka:o_va]), dup_heads(bias[o_va:o_z])])
        w_misc = jnp.pad(w[:, o_z:o_qc], ((0, 0), (0, dt_pad))).astype(BF16)
        b_misc = jnp.pad(bias[o_z:o_qc], (0, dt_pad))

        qkv_a = _proj(xb, w_a, b_a, BF16, tm, w_a.shape[1], seq, rope=rope, rope_blocks=(A_Q + A_KV_DUP) // LANES, name="proj_swa_qkv")
        qk_c = _proj(xb, w[:, o_qc:o_vc].astype(BF16), bias[o_qc:o_vc], F32, tm, C_QKV, seq, rope=rope, rope_blocks=C_QKV // LANES, name="proj_dil_qk")
        v_c = _proj(xb, w[:, o_vc:o_gate].astype(BF16), bias[o_vc:o_gate], F32, tm, C_QKV, seq, name="proj_dil_v")
        misc = _proj(xb, w_misc, b_misc, F32, tm, w_misc.shape[1] // 3, seq, name="proj_ssd")
        gates = _proj(xb, w[:, o_gate:o_end].astype(BF16), bias[o_gate:o_end], F32, tm, 1024, seq, act="sigmoid", name="proj_gates")

        y_swa = _swa_attention(qkv_a, attn_sinks[l], batch, seq)
        y_ssd = _ssd_mixer(misc, conv_w[l], conv_b[l], dt_bias[l], a_log[l], d_skip[l], ssd_norm_w[l], batch, seq)
        dil = [_dilated_attention(qk_c, v_c, batch, seq, gi) for gi in range(len(DIL_CONFIGS))]

        merged = _merge(y_swa, y_ssd, dil, gates, proj_swa[l].astype(BF16), proj_ssd[l].astype(BF16),
                        proj_dil[l].astype(BF16), tm, 1024)

        wr = jnp.concatenate([router_group_w[l], router_expert_w[l].transpose(1, 0, 2).reshape(d, N_EXPERTS)], axis=1)
        wr = jnp.pad(wr, ((0, 0), (0, LANES - wr.shape[1])))
        br = jnp.pad(jnp.concatenate([router_group_b[l], router_expert_b[l].reshape(N_EXPERTS)]), (0, LANES - N_GROUPS - N_EXPERTS))
        x1, ids, comb, cnt = _ln1_router(xf, merged, w_out[l].astype(BF16), ln1_g[l], ln1_b[l], wr.astype(BF16), br.reshape(1, LANES), alpha, tm)

        plan = _dispatch_plan(ids, cnt[0, N_GROUPS:N_GROUPS + N_EXPERTS].astype(I32), t)
        y2 = _moe_experts(x1, plan, expert_w1[l].astype(BF16), expert_w3[l].astype(BF16), expert_w2[l].astype(BF16))
        xf, xb = _ln2(x1, y2, comb, ln2_g[l], ln2_b[l], alpha, tm)
    return xf.reshape(batch, seq, d)
```

```python
import functools

import jax
import jax.numpy as jnp
from jax import lax
from jax.experimental import pallas as pl
from jax.experimental.pallas import tpu as pltpu

F32 = jnp.float32
BF16 = jnp.bfloat16
I32 = jnp.int32

LANES = 128
SUBLANES = 8
VMEM_LIMIT_BYTES = 48 * 1024 * 1024

HEAD_DIM = 64
HALF_HEAD = HEAD_DIM // 2
ROPE_THETA = 10000.0
ATTN_BLOCK = 128
LN_EPS = 1e-5

SWA_Q_HEADS = 16
SWA_KV_HEADS = 2
SWA_WINDOW = 128

SSD_HEADS = 24
SSD_HEADDIM = 64
SSD_INNER = SSD_HEADS * SSD_HEADDIM
SSD_GROUPS = 4
SSD_STATE = 128
SSD_CONV = 4
SSD_CHUNK = 128
SSD_HEADS_PER_GROUP = SSD_HEADS // SSD_GROUPS
SSD_GROUP_WIDTH = SSD_INNER // SSD_GROUPS
SSD_BC_WIDTH = 2 * SSD_GROUPS * SSD_STATE

DIL_CONFIGS = ((128, 1), (512, 4), (2048, 16))
DIL_HEADS_PER_GROUP = 6
DIL_HEADS = DIL_HEADS_PER_GROUP * len(DIL_CONFIGS)

A_Q = SWA_Q_HEADS * HEAD_DIM
A_KV = SWA_KV_HEADS * HEAD_DIM
A_KV_DUP = 2 * A_KV
C_QKV = DIL_HEADS * HEAD_DIM
C_OUT = DIL_HEADS_PER_GROUP * HEAD_DIM

N_GROUPS = 4
EXPERTS_PER_GROUP = 8
N_EXPERTS = N_GROUPS * EXPERTS_PER_GROUP
TOP_K = 2
D_EXPERT = 512
MOE_BLOCK = 128
MOE_DMA_UNROLL = 8

NEG_INF = float("-inf")


def _cparams(*sem):
    return pltpu.CompilerParams(dimension_semantics=sem, vmem_limit_bytes=VMEM_LIMIT_BYTES)


def _sigmoid(v):
    return 1.0 / (1.0 + jnp.exp(-v))


def _proj_kernel(*refs, rope_blocks, dup_from, act):
    if rope_blocks:
        x_ref, w_ref, b_ref, cos_ref, sin_ref, o_ref = refs
    else:
        x_ref, w_ref, b_ref, o_ref = refs
    acc = jnp.dot(x_ref[...], w_ref[...], preferred_element_type=F32) + b_ref[...]
    n_blocks = acc.shape[1] // LANES
    if rope_blocks or dup_from < n_blocks:
        lane = lax.broadcasted_iota(I32, (acc.shape[0], LANES), 1)
        first_half = (lane % HEAD_DIM) < HALF_HEAD
        low_head = lane < HEAD_DIM
        out_c = 0
        for c in range(n_blocks):
            v = acc[:, c * LANES:(c + 1) * LANES]
            if c < rope_blocks:
                partner = jnp.where(first_half, pltpu.roll(v, LANES - HALF_HEAD, 1), pltpu.roll(v, HALF_HEAD, 1))
                v = v * cos_ref[...] + partner * sin_ref[...]
            if c >= dup_from:
                other = pltpu.roll(v, HEAD_DIM, 1)
                outs = (jnp.where(low_head, v, other), jnp.where(low_head, other, v))
            else:
                outs = (v,)
            for o in outs:
                o_ref[:, out_c * LANES:(out_c + 1) * LANES] = o.astype(o_ref.dtype)
                out_c += 1
    elif act == "sigmoid":
        o_ref[...] = _sigmoid(acc).astype(o_ref.dtype)
    else:
        o_ref[...] = acc.astype(o_ref.dtype)


def _proj(xb, w, b, layer, col0, n, out_dtype, tm, tn, seq, rope=None, rope_blocks=0, dup_from=None, act=None, name="proj"):
    t, d = xb.shape
    assert t % tm == 0 and n % tn == 0 and seq % tm == 0 and col0 % tn == 0
    j0 = col0 // tn
    in_blocks = tn // LANES
    if dup_from is None:
        dup_from = in_blocks
    assert dup_from == in_blocks or n == tn
    tn_out = tn + (in_blocks - dup_from) * LANES
    in_specs = [
        pl.BlockSpec((tm, d), lambda j, i: (i, 0)),
        pl.BlockSpec((None, d, tn), lambda j, i: (layer, 0, j0 + j)),
        pl.BlockSpec((None, 1, tn), lambda j, i: (layer, 0, j0 + j)),
    ]
    args = [xb, w, b]
    if rope_blocks:
        sblocks = seq // tm
        in_specs += [pl.BlockSpec((tm, LANES), lambda j, i: (i % sblocks, 0))] * 2
        args += list(rope)
    return pl.pallas_call(
        functools.partial(_proj_kernel, rope_blocks=rope_blocks, dup_from=dup_from, act=act),
        grid=(n // tn, t // tm),
        in_specs=in_specs,
        out_specs=pl.BlockSpec((tm, tn_out), lambda j, i: (i, j)),
        out_shape=jax.ShapeDtypeStruct((t, n // tn * tn_out), out_dtype),
        compiler_params=_cparams("arbitrary", "arbitrary"),
        name=name,
    )(*args)


def _band_biases(max_dist):
    row = lax.broadcasted_iota(I32, (ATTN_BLOCK, 2 * ATTN_BLOCK), 0)
    col = lax.broadcasted_iota(I32, (ATTN_BLOCK, 2 * ATTN_BLOCK), 1)
    dist = row + ATTN_BLOCK - col
    in_band = (dist >= 0) & (dist <= max_dist)
    bias_any = jnp.where(in_band, 0.0, NEG_INF).astype(F32)
    bias_first = jnp.where(in_band & (col >= ATTN_BLOCK), 0.0, NEG_INF).astype(F32)
    return bias_any, bias_first


def _attend_pair(q2, k2, v2, bias, sinks):
    lane = lax.broadcasted_iota(I32, (ATTN_BLOCK, LANES), 1)
    low = lane < HEAD_DIM
    scale = HEAD_DIM ** -0.5
    outs, lses = [], []
    for half in range(2):
        qh = jnp.where(low if half == 0 else jnp.logical_not(low), q2, jnp.zeros_like(q2))
        s = lax.dot_general(qh, k2, (((1,), (1,)), ((), ())), preferred_element_type=F32) * scale
        s = s + bias
        m = jnp.max(s, axis=-1, keepdims=True)
        if sinks is not None:
            m = jnp.maximum(m, sinks[half])
        p = jnp.exp(s - m)
        den = jnp.sum(p, axis=-1, keepdims=True)
        if sinks is not None:
            den = den + jnp.exp(sinks[half] - m)
        outs.append(jnp.dot(p.astype(BF16), v2, preferred_element_type=F32) / den)
        lses.append(m + jnp.log(den))
    return jnp.where(low, outs[0], outs[1]), jnp.where(low, lses[0], lses[1])


def _swa_attn_kernel(sink_ref, q_ref, kc_ref, kp_ref, vc_ref, vp_ref, o_ref, kf, vf, *, pairs_per_kv, tq):
    n = pl.program_id(1)
    kf[0:ATTN_BLOCK, :] = kp_ref[...]
    kf[ATTN_BLOCK:, :] = kc_ref[...]
    vf[0:ATTN_BLOCK, :] = vp_ref[...]
    vf[ATTN_BLOCK:, :] = vc_ref[...]
    bias_any, bias_first = _band_biases(SWA_WINDOW - 1)

    def sub_block(j, carry):
        r0 = pl.multiple_of(j * ATTN_BLOCK, ATTN_BLOCK)
        bias = jnp.where(jnp.logical_and(n == 0, j == 0), bias_first, bias_any)
        for hp in range(SWA_Q_HEADS // 2):
            cols = slice(hp * LANES, (hp + 1) * LANES)
            kv_cols = slice((hp // pairs_per_kv) * LANES, (hp // pairs_per_kv + 1) * LANES)
            o, _ = _attend_pair(q_ref[pl.ds(r0, ATTN_BLOCK), cols], kf[pl.ds(r0, 2 * ATTN_BLOCK), kv_cols],
                                vf[pl.ds(r0, 2 * ATTN_BLOCK), kv_cols], bias, (sink_ref[2 * hp], sink_ref[2 * hp + 1]))
            o_ref[pl.ds(r0, ATTN_BLOCK), cols] = o.astype(o_ref.dtype)
        return carry

    lax.fori_loop(0, tq // ATTN_BLOCK, sub_block, 0)


def _dil_attn_kernel(q_ref, kc_ref, kp_ref, vc_ref, vp_ref, o_ref, l_ref, *, dil, max_dist):
    n = pl.program_id(1)
    rows = q_ref.shape[0]
    sub_rows = rows // dil
    bias_any, bias_first = _band_biases(max_dist)
    bias_head = jnp.where(n == 0, bias_first, bias_any)

    def rows_of(ref, r, start, count):
        if dil == 1:
            return ref[start:start + count, :]
        return ref[pl.ds(start * dil + r, count, stride=dil), :]

    for r in range(dil):
        k_prev = rows_of(kp_ref, r, 0, ATTN_BLOCK).astype(BF16)
        v_prev = rows_of(vp_ref, r, 0, ATTN_BLOCK).astype(BF16)
        for j in range(sub_rows // ATTN_BLOCK):
            start = j * ATTN_BLOCK
            k_cur = rows_of(kc_ref, r, start, ATTN_BLOCK).astype(BF16)
            v_cur = rows_of(vc_ref, r, start, ATTN_BLOCK).astype(BF16)
            q2 = rows_of(q_ref, r, start, ATTN_BLOCK).astype(BF16)
            o, l = _attend_pair(q2, jnp.concatenate([k_prev, k_cur], axis=0), jnp.concatenate([v_prev, v_cur], axis=0),
                                bias_head if j == 0 else bias_any, None)
            if dil == 1:
                o_ref[start:start + ATTN_BLOCK, :] = o
                l_ref[start:start + ATTN_BLOCK, :] = l
            else:
                o_ref[pl.ds(start * dil + r, ATTN_BLOCK, stride=dil), :] = o
                l_ref[pl.ds(start * dil + r, ATTN_BLOCK, stride=dil), :] = l
            k_prev, v_prev = k_cur, v_cur


def _attn_tile(length):
    return min(512, length)


def _swa_attention(qkv, sinks, batch, seq):
    tq = _attn_tile(seq)
    qkv = qkv.reshape(batch, seq, A_Q + 2 * A_KV_DUP)
    k_blk = A_Q // A_KV_DUP
    v_blk = k_blk + 1
    sub = tq // ATTN_BLOCK

    def prev(n):
        return jnp.maximum(n * sub - 1, 0)

    kern = functools.partial(_swa_attn_kernel, pairs_per_kv=SWA_Q_HEADS // SWA_KV_HEADS // 2, tq=tq)
    return pl.pallas_call(
        kern,
        grid=(batch, seq // tq),
        in_specs=[
            pl.BlockSpec(memory_space=pltpu.SMEM),
            pl.BlockSpec((None, tq, A_Q), lambda b, n: (b, n, 0)),
            pl.BlockSpec((None, tq, A_KV_DUP), lambda b, n: (b, n, k_blk)),
            pl.BlockSpec((None, ATTN_BLOCK, A_KV_DUP), lambda b, n: (b, prev(n), k_blk)),
            pl.BlockSpec((None, tq, A_KV_DUP), lambda b, n: (b, n, v_blk)),
            pl.BlockSpec((None, ATTN_BLOCK, A_KV_DUP), lambda b, n: (b, prev(n), v_blk)),
        ],
        out_specs=pl.BlockSpec((None, tq, A_Q), lambda b, n: (b, n, 0)),
        out_shape=jax.ShapeDtypeStruct((batch, seq, A_Q), BF16),
        scratch_shapes=[pltpu.VMEM((tq + ATTN_BLOCK, A_KV_DUP), BF16), pltpu.VMEM((tq + ATTN_BLOCK, A_KV_DUP), BF16)],
        compiler_params=_cparams("arbitrary", "arbitrary"),
        name="swa_attn",
    )(sinks, qkv, qkv, qkv, qkv, qkv).reshape(batch * seq, A_Q)


def _dilated_attention(qk, v, batch, seq, gi):
    window, dil = DIL_CONFIGS[gi]
    back = ATTN_BLOCK * dil
    rows = max(_attn_tile(seq), back)
    assert seq % rows == 0 and rows % back == 0
    pairs = DIL_HEADS_PER_GROUP // 2
    q_blk, k_blk, v_blk = gi * pairs, (C_QKV // LANES) + gi * pairs, gi * pairs
    per_seq = seq // rows
    ratio = rows // back

    def cur(col0):
        return pl.BlockSpec((rows, LANES), lambda b, n, hp: (b * per_seq + n, col0 + hp))

    def prev(col0):
        return pl.BlockSpec((back, LANES), lambda b, n, hp: (b * per_seq * ratio + jnp.maximum(n * ratio - 1, 0), col0 + hp))

    out_sds = jax.ShapeDtypeStruct((batch * seq, C_OUT), F32)
    out_spec = pl.BlockSpec((rows, LANES), lambda b, n, hp: (b * per_seq + n, hp))
    return pl.pallas_call(
        functools.partial(_dil_attn_kernel, dil=dil, max_dist=window // dil),
        grid=(batch, per_seq, pairs),
        in_specs=[cur(q_blk), cur(k_blk), prev(k_blk), cur(v_blk), prev(v_blk)],
        out_specs=[out_spec, out_spec],
        out_shape=[out_sds, out_sds],
        compiler_params=_cparams("arbitrary", "arbitrary", "arbitrary"),
        name=f"dil_attn_{dil}",
    )(qk, qk, qk, v, v)


def _ssd_kernel(z_ref, xs_ref, bc_ref, dt_ref, cwx_ref, cwbc_ref, cbx_ref, cbbc_ref, dtb_ref, alog_ref,
                dsk_ref, nw_ref, o_ref, extx, extbc, hst):
    q = SSD_CHUNK
    c = pl.program_id(1)

    @pl.when(c == 0)
    def _():
        extx[0:SUBLANES, :] = jnp.zeros((SUBLANES, SSD_INNER), F32)
        extbc[0:SUBLANES, :] = jnp.zeros((SUBLANES, SSD_BC_WIDTH), F32)
        hst[...] = jnp.zeros(hst.shape, F32)

    extx[SUBLANES:, :] = xs_ref[...]
    extbc[SUBLANES:, :] = bc_ref[...]

    def conv_silu(ext, cw_ref, cb_ref):
        acc = cb_ref[...]
        for k in range(SSD_CONV):
            start = SUBLANES - (SSD_CONV - 1) + k
            acc = acc + cw_ref[k:k + 1, :] * ext[start:start + q, :]
        return acc * _sigmoid(acc)

    xa = conv_silu(extx, cwx_ref, cbx_ref)
    bca = conv_silu(extbc, cwbc_ref, cbbc_ref)
    extx[0:SUBLANES, :] = xs_ref[q - SUBLANES:q, :]
    extbc[0:SUBLANES, :] = bc_ref[q - SUBLANES:q, :]

    dtv = dt_ref[...] + dtb_ref[...]
    dt = jnp.maximum(dtv, 0.0) + jnp.log(1.0 + jnp.exp(-jnp.abs(dtv)))
    a = -jnp.exp(alog_ref[...])
    rr = lax.broadcasted_iota(I32, (q, q), 0)
    cc = lax.broadcasted_iota(I32, (q, q), 1)
    causal = rr >= cc
    tril = jnp.where(causal, 1.0, 0.0).astype(F32)
    acs = jnp.dot(tril, dt * a, preferred_element_type=F32, precision=lax.Precision.HIGHEST)
    acs_t = acs.T

    low = lax.broadcasted_iota(I32, (q, LANES), 1) < SSD_HEADDIM
    low_state = lax.broadcasted_iota(I32, (SSD_STATE, LANES), 1) < SSD_HEADDIM
    pairs_per_group = SSD_HEADS_PER_GROUP // 2

    def per_head(h0, fn):
        return jnp.where(low, fn(h0), fn(h0 + 1))

    for g in range(SSD_GROUPS):
        bg = bca[:, g * SSD_STATE:(g + 1) * SSD_STATE]
        cg = bca[:, (SSD_GROUPS + g) * SSD_STATE:(SSD_GROUPS + g + 1) * SSD_STATE].astype(BF16)
        bt = bg.T.astype(BF16)
        cb = jnp.dot(cg, bt, preferred_element_type=F32)
        ys = []
        for pi in range(pairs_per_group):
            pair = g * pairs_per_group + pi
            h0 = 2 * pair
            cols = slice(pair * LANES, (pair + 1) * LANES)
            xp = xa[:, cols]
            xdt = xp * per_head(h0, lambda h: dt[:, h:h + 1])
            m_both = jnp.concatenate(
                [(cb * jnp.exp(jnp.where(causal, acs[:, h:h + 1] - acs_t[h:h + 1, :], NEG_INF))).astype(BF16)
                 for h in (h0, h0 + 1)], axis=1)
            x_both = jnp.concatenate([jnp.where(low, xdt, 0.0), jnp.where(low, 0.0, xdt)], axis=0).astype(BF16)
            y = jnp.dot(m_both, x_both, preferred_element_type=F32)
            hprev = hst[pair]
            y = y + jnp.dot(cg, hprev.astype(BF16), preferred_element_type=F32) * per_head(h0, lambda h: jnp.exp(acs[:, h:h + 1]))
            y = y + xp * dsk_ref[:, cols]
            to_end = per_head(h0, lambda h: jnp.exp(acs[q - 1:q, h:h + 1] - acs[:, h:h + 1]))
            chunk_decay = jnp.where(low_state, jnp.exp(acs[q - 1:q, h0:h0 + 1]), jnp.exp(acs[q - 1:q, h0 + 1:h0 + 2]))
            hst[pair] = hprev * chunk_decay + jnp.dot(bt, (xdt * to_end).astype(BF16), preferred_element_type=F32)
            ys.append(y)
        lo, hi = g * SSD_GROUP_WIDTH, (g + 1) * SSD_GROUP_WIDTH
        zg = z_ref[:, lo:hi]
        yg = jnp.concatenate(ys, axis=1) * (zg * _sigmoid(zg))
        yg = yg * lax.rsqrt(jnp.mean(yg * yg, axis=-1, keepdims=True) + LN_EPS)
        o_ref[:, lo:hi] = (yg * nw_ref[:, lo:hi]).astype(o_ref.dtype)


def _ssd_mixer(misc, conv_w, conv_b, dt_bias, a_log, d_skip, norm_w, batch, seq):
    q = SSD_CHUNK
    nc = seq // q
    pad = LANES - SSD_HEADS
    row = lambda v: v.reshape(1, -1)
    args = [
        misc, misc, misc, misc,
        conv_w[:, :SSD_INNER], conv_w[:, SSD_INNER:], row(conv_b[:SSD_INNER]), row(conv_b[SSD_INNER:]),
        row(jnp.pad(dt_bias, (0, pad))), row(jnp.pad(a_log, (0, pad))),
        row(jnp.repeat(d_skip, SSD_HEADDIM)), row(norm_w),
    ]
    tok = lambda b, c: b * nc + c
    full = lambda shape: pl.BlockSpec(shape, lambda b, c: (0, 0))
    bc_blk = 2 * SSD_INNER // SSD_BC_WIDTH
    dt_blk = (2 * SSD_INNER + SSD_BC_WIDTH) // LANES
    return pl.pallas_call(
        _ssd_kernel,
        grid=(batch, nc),
        in_specs=[
            pl.BlockSpec((q, SSD_INNER), lambda b, c: (tok(b, c), 0)),
            pl.BlockSpec((q, SSD_INNER), lambda b, c: (tok(b, c), 1)),
            pl.BlockSpec((q, SSD_BC_WIDTH), lambda b, c: (tok(b, c), bc_blk)),
            pl.BlockSpec((q, LANES), lambda b, c: (tok(b, c), dt_blk)),
            full((SSD_CONV, SSD_INNER)), full((SSD_CONV, SSD_BC_WIDTH)), full((1, SSD_INNER)), full((1, SSD_BC_WIDTH)),
            full((1, LANES)), full((1, LANES)), full((1, SSD_INNER)), full((1, SSD_INNER)),
        ],
        out_specs=pl.BlockSpec((q, SSD_INNER), lambda b, c: (tok(b, c), 0)),
        out_shape=jax.ShapeDtypeStruct((batch * seq, SSD_INNER), BF16),
        scratch_shapes=[
            pltpu.VMEM((SUBLANES + q, SSD_INNER), F32),
            pltpu.VMEM((SUBLANES + q, SSD_BC_WIDTH), F32),
            pltpu.VMEM((SSD_HEADS // 2, SSD_STATE, LANES), F32),
        ],
        compiler_params=_cparams("arbitrary", "arbitrary"),
        name="ssd_mixer",
    )(*args)


def _merge_kernel(ysw_ref, yss_ref, o0_ref, l0_ref, o1_ref, l1_ref, o2_ref, l2_ref, g0_ref, g1_ref, g2_ref,
                  p0_ref, p1_ref, p2_ref, out_ref):
    l0, l1, l2 = l0_ref[...], l1_ref[...], l2_ref[...]
    m = jnp.maximum(jnp.maximum(l0, l1), l2)
    e0, e1, e2 = jnp.exp(l0 - m), jnp.exp(l1 - m), jnp.exp(l2 - m)
    inv = 1.0 / (e0 + e1 + e2)
    ydil = (o0_ref[...] * (e0 * inv) + o1_ref[...] * (e1 * inv) + o2_ref[...] * (e2 * inv)).astype(BF16)
    merged = g0_ref[...] * jnp.dot(ysw_ref[...], p0_ref[...], preferred_element_type=F32)
    merged = merged + g1_ref[...] * jnp.dot(yss_ref[...], p1_ref[...], preferred_element_type=F32)
    merged = merged + g2_ref[...] * jnp.dot(ydil, p2_ref[...], preferred_element_type=F32)
    out_ref[...] = merged.astype(out_ref.dtype)


def _merge(y_swa, y_ssd, dil, gates, p_swa, p_ssd, p_dil, tm, tn):
    t = y_swa.shape[0]
    d = p_swa.shape[1]
    nj = d // tn
    rows = lambda width: pl.BlockSpec((tm, width), lambda i, j: (i, 0))
    gate = lambda k: pl.BlockSpec((tm, tn), lambda i, j: (i, k * nj + j))
    wcol = lambda width: pl.BlockSpec((width, tn), lambda i, j: (0, j))
    dil_args = [a for pair in dil for a in pair]
    return pl.pallas_call(
        _merge_kernel,
        grid=(t // tm, nj),
        in_specs=[rows(A_Q), rows(SSD_INNER)] + [rows(C_OUT)] * 6 + [gate(0), gate(1), gate(2)]
        + [wcol(A_Q), wcol(SSD_INNER), wcol(C_OUT)],
        out_specs=pl.BlockSpec((tm, tn), lambda i, j: (i, j)),
        out_shape=jax.ShapeDtypeStruct((t, d), BF16),
        compiler_params=_cparams("arbitrary", "arbitrary"),
        name="merge",
    )(y_swa, y_ssd, *dil_args, gates, gates, gates, p_swa, p_ssd, p_dil)


def _layer_norm(h, g, b):
    mu = jnp.mean(h, axis=-1, keepdims=True)
    hc = h - mu
    var = jnp.mean(hc * hc, axis=-1, keepdims=True)
    return hc * lax.rsqrt(var + LN_EPS) * g + b


def _ln1_router_kernel(x_ref, m_ref, w_ref, g_ref, b_ref, wr_ref, br_ref, xo_ref, ids_ref, comb_ref, cnt_ref, *, alpha):
    i = pl.program_id(0)
    h = alpha * x_ref[...] + jnp.dot(m_ref[...], w_ref[...], preferred_element_type=F32)
    y = _layer_norm(h, g_ref[...], b_ref[...])
    xo_ref[...] = y

    logits = jnp.dot(y.astype(BF16), wr_ref[...], preferred_element_type=F32) + br_ref[...]
    lane = lax.broadcasted_iota(I32, logits.shape, 1).astype(F32)
    big = float(2 * LANES)

    def first_argmax(vals):
        top = jnp.max(vals, axis=-1, keepdims=True)
        return top, jnp.min(jnp.where(vals == top, lane, big), axis=-1, keepdims=True)

    gl = jnp.where(lane < N_GROUPS, logits, NEG_INF)
    gmax, gsel = first_argmax(gl)
    p_group = 1.0 / jnp.sum(jnp.exp(gl - gmax), axis=-1, keepdims=True)
    lo = N_GROUPS + gsel * EXPERTS_PER_GROUP
    el = jnp.where((lane >= lo) & (lane < lo + EXPERTS_PER_GROUP), logits, NEG_INF)
    v1, i1 = first_argmax(el)
    v2, i2 = first_argmax(jnp.where(lane == i1, NEG_INF, el))
    e21 = jnp.exp(v2 - v1)
    inv = 1.0 / (1.0 + e21)
    comb_ref[...] = jnp.where(lane == 0.0, inv * p_group, jnp.where(lane == 1.0, e21 * inv * p_group, 0.0))

    @pl.when(i == 0)
    def _():
        cnt_ref[...] = jnp.zeros(cnt_ref.shape, F32)

    pick1 = jnp.where(lane == i1, 1.0, 0.0)
    pick2 = jnp.where(lane == i2, 1.0, 0.0)
    picks = pick1 + pick2
    tm = picks.shape[0]
    earlier = lax.broadcasted_iota(I32, (tm, tm), 0) > lax.broadcasted_iota(I32, (tm, tm), 1)
    before = jnp.dot(jnp.where(earlier, 1.0, 0.0).astype(BF16), picks.astype(BF16), preferred_element_type=F32) + cnt_ref[...]
    rank1 = jnp.sum(pick1 * before, axis=-1, keepdims=True)
    rank2 = jnp.sum(pick2 * before, axis=-1, keepdims=True)
    cnt_ref[...] = cnt_ref[...] + jnp.sum(picks, axis=0, keepdims=True)
    ids = jnp.where(lane == 0.0, i1 - N_GROUPS, jnp.where(lane == 1.0, i2 - N_GROUPS,
                    jnp.where(lane == 2.0, rank1, jnp.where(lane == 3.0, rank2, 0.0))))
    for c in range(tm // LANES):
        ids_ref[:, c * LANES:(c + 1) * LANES] = ids[c * LANES:(c + 1) * LANES, :].T[0:SUBLANES, :].astype(I32)


def _ln1_router(x, merged, w_out, g, b, wr, br, alpha, tm):
    t, d = x.shape
    rows = lambda width: pl.BlockSpec((tm, width), lambda i: (i, 0))
    full = lambda shape: pl.BlockSpec(shape, lambda i: (0, 0))
    once = lambda shape: pl.BlockSpec(shape, lambda i: (0, 0), pipeline_mode=pl.Buffered(1))
    return pl.pallas_call(
        functools.partial(_ln1_router_kernel, alpha=alpha),
        grid=(t // tm,),
        in_specs=[rows(d), rows(d), once((d, d)), full((1, d)), full((1, d)), once((d, LANES)), full((1, LANES))],
        out_specs=[rows(d), pl.BlockSpec((SUBLANES, tm), lambda i: (0, i)), rows(LANES), full((1, LANES))],
        out_shape=[jax.ShapeDtypeStruct((t, d), F32),
                   jax.ShapeDtypeStruct((SUBLANES, t), I32), jax.ShapeDtypeStruct((t, LANES), F32),
                   jax.ShapeDtypeStruct((1, LANES), F32)],
        compiler_params=_cparams("arbitrary"),
        name="ln1_router",
    )(x, merged, w_out, g.reshape(1, d), b.reshape(1, d), wr, br)


def _moe_kernel(be_ref, dest_ref, cnt_ref, pad_lo_ref, pad_hi_ref, nu_ref, x_hbm, w1_ref, w3_ref, w2_ref, y_hbm,
                xbuf, ybuf, w1b, w3b, w2b, tok_ref, dst_ref, gsem, ssem, *, n_tok):
    b = pl.program_id(0)
    n_used = nu_ref[0]

    @pl.when(b == 0)
    def _():
        def pad_expert(e, carry):
            def pad_slot(s, c):
                tok_ref[s] = 0
                return c
            lax.fori_loop(pad_lo_ref[e], pad_hi_ref[e], pad_slot, 0)
            return carry
        lax.fori_loop(0, N_EXPERTS, pad_expert, 0)

        for k in range(TOP_K):
            def fill(c, carry):
                for u in range(MOE_DMA_UNROLL):
                    tkn = c * MOE_DMA_UNROLL + u
                    s = dest_ref[k * n_tok + tkn]
                    tok_ref[s] = tkn
                    dst_ref[s] = k * n_tok + tkn
                return carry
            lax.fori_loop(0, n_tok // MOE_DMA_UNROLL, fill, 0)

    def gather_copy(blk, slot, r):
        return pltpu.make_async_copy(
            x_hbm.at[pl.ds(tok_ref[blk * MOE_BLOCK + r], 1), :], xbuf.at[slot, pl.ds(r, 1), :], gsem.at[slot])

    def scatter_copy(blk, r):
        return pltpu.make_async_copy(
            ybuf.at[pl.ds(r, 1), :], y_hbm.at[pl.ds(dst_ref[blk * MOE_BLOCK + r], 1), :], ssem.at[0])

    def for_rows(count, fn):
        def chunk(c, carry):
            for u in range(MOE_DMA_UNROLL):
                fn(c * MOE_DMA_UNROLL + u)
            return carry

        def single(r, carry):
            fn(r)
            return carry

        if isinstance(count, int):
            lax.fori_loop(0, count // MOE_DMA_UNROLL, chunk, 0)
        else:
            whole = count // MOE_DMA_UNROLL
            lax.fori_loop(0, whole, chunk, 0)
            lax.fori_loop(whole * MOE_DMA_UNROLL, count, single, 0)

    @pl.when(b < n_used)
    def _():
        slot = b % 2

        @pl.when(b == 0)
        def _():
            for_rows(MOE_BLOCK, lambda r: gather_copy(0, 0, r).start())

        @pl.when(b + 1 < n_used)
        def _():
            for_rows(MOE_BLOCK, lambda r: gather_copy(b + 1, 1 - slot, r).start())

        @pl.when(jnp.logical_or(b == 0, be_ref[b] != be_ref[jnp.maximum(b - 1, 0)]))
        def _():
            w1b[...] = w1_ref[...].astype(BF16)
            w3b[...] = w3_ref[...].astype(BF16)
            w2b[...] = w2_ref[...].astype(BF16)

        for_rows(MOE_BLOCK, lambda r: gather_copy(b, slot, r).wait())
        xb = xbuf[slot].astype(BF16)
        h1 = jnp.dot(xb, w1b[...], preferred_element_type=F32)
        h3 = jnp.dot(xb, w3b[...], preferred_element_type=F32)
        h = (h1 * _sigmoid(h1) * h3).astype(BF16)
        y = jnp.dot(h, w2b[...], preferred_element_type=F32)

        @pl.when(b > 0)
        def _():
            for_rows(cnt_ref[b - 1], lambda r: scatter_copy(b - 1, r).wait())

        ybuf[...] = y
        for_rows(cnt_ref[b], lambda r: scatter_copy(b, r).start())

        @pl.when(b == n_used - 1)
        def _():
            for_rows(cnt_ref[b], lambda r: scatter_copy(b, r).wait())


def _moe_experts(x1, plan, w1, w3, w2, layer):
    t, d = x1.shape
    block_e, dest, block_cnt, pad_lo, pad_hi, n_used = plan
    n_blocks = block_e.shape[0]
    wspec = lambda shape: pl.BlockSpec((None, None) + shape, lambda b, be, *_: (layer, be[b], 0, 0))
    grid_spec = pltpu.PrefetchScalarGridSpec(
        num_scalar_prefetch=6,
        grid=(n_blocks,),
        in_specs=[pl.BlockSpec(memory_space=pl.ANY), wspec((d, D_EXPERT)), wspec((d, D_EXPERT)), wspec((D_EXPERT, d))],
        out_specs=pl.BlockSpec(memory_space=pl.ANY),
        scratch_shapes=[
            pltpu.VMEM((2, MOE_BLOCK, d), F32),
            pltpu.VMEM((MOE_BLOCK, d), F32),
            pltpu.VMEM((d, D_EXPERT), BF16),
            pltpu.VMEM((d, D_EXPERT), BF16),
            pltpu.VMEM((D_EXPERT, d), BF16),
            pltpu.SMEM((n_blocks * MOE_BLOCK,), I32),
            pltpu.SMEM((n_blocks * MOE_BLOCK,), I32),
            pltpu.SemaphoreType.DMA((2,)),
            pltpu.SemaphoreType.DMA((1,)),
        ],
    )
    return pl.pallas_call(
        functools.partial(_moe_kernel, n_tok=t),
        grid_spec=grid_spec,
        out_shape=jax.ShapeDtypeStruct((TOP_K * t, d), F32),
        compiler_params=_cparams("arbitrary"),
        name="moe_experts",
    )(block_e, dest, block_cnt, pad_lo, pad_hi, n_used, x1, w1, w3, w2)


def _ln2_kernel(x_ref, y0_ref, y1_ref, comb_ref, g_ref, b_ref, xo_ref, xb_ref, *, alpha):
    comb = comb_ref[...]
    h = alpha * x_ref[...] + comb[:, 0:1] * y0_ref[...] + comb[:, 1:2] * y1_ref[...]
    y = _layer_norm(h, g_ref[...], b_ref[...])
    xo_ref[...] = y
    xb_ref[...] = y.astype(BF16)


def _ln2(x1, y2, comb, g, b, alpha, tm):
    t, d = x1.shape
    nblk = t // tm
    rows = lambda width: pl.BlockSpec((tm, width), lambda i: (i, 0))
    full = lambda shape: pl.BlockSpec(shape, lambda i: (0, 0))
    return pl.pallas_call(
        functools.partial(_ln2_kernel, alpha=alpha),
        grid=(nblk,),
        in_specs=[rows(d), rows(d), pl.BlockSpec((tm, d), lambda i: (nblk + i, 0)), rows(LANES), full((1, d)), full((1, d))],
        out_specs=[rows(d), rows(d)],
        out_shape=[jax.ShapeDtypeStruct((t, d), F32), jax.ShapeDtypeStruct((t, d), BF16)],
        compiler_params=_cparams("arbitrary"),
        name="ln2",
    )(x1, y2, y2, comb, g.reshape(1, d), b.reshape(1, d))


def _dispatch_plan(ids, counts, n_tok):
    n_assign = n_tok * TOP_K
    n_blocks = -(-n_assign // MOE_BLOCK) + N_EXPERTS
    experts = jnp.arange(N_EXPERTS, dtype=I32)
    padded = (counts + MOE_BLOCK - 1) // MOE_BLOCK * MOE_BLOCK
    pad_end = jnp.cumsum(padded)
    pad_start = pad_end - padded

    def lookup(table, idx):
        return jnp.sum(jnp.where(idx[..., None] == experts, table, 0), axis=-1)

    dest = (lookup(pad_start, ids[0:TOP_K]) + ids[TOP_K:2 * TOP_K]).reshape(n_assign)
    starts = jnp.arange(n_blocks, dtype=I32) * MOE_BLOCK
    block_e = jnp.minimum(jnp.sum((pad_end[None, :] <= starts[:, None]).astype(I32), axis=1), N_EXPERTS - 1)
    block_cnt = jnp.clip(lookup(counts, block_e) - (starts - lookup(pad_start, block_e)), 0, MOE_BLOCK)
    n_used = pad_end[-1:] // MOE_BLOCK
    return block_e, dest, block_cnt, pad_start + counts, pad_end, n_used


def _rope_tables(seq):
    inv_freq = 1.0 / (ROPE_THETA ** (jnp.arange(0, HEAD_DIM, 2, dtype=F32) / HEAD_DIM))
    ang = jnp.arange(seq, dtype=F32)[:, None] * inv_freq[None, :]
    cos, sin = jnp.cos(ang), jnp.sin(ang)
    reps = LANES // HEAD_DIM
    return jnp.tile(jnp.concatenate([cos, cos], axis=1), (1, reps)), jnp.tile(jnp.concatenate([-sin, sin], axis=1), (1, reps))


def _tile(total, want):
    return min(total, want)


def kernel(x, w_in, b_in, attn_sinks, conv_w, conv_b, dt_bias, a_log, d_skip, ssd_norm_w, proj_swa, proj_ssd, proj_dil, w_out, ln1_g, ln1_b, router_group_w, router_group_b, router_expert_w, router_expert_b, expert_w1, expert_w3, expert_w2, ln2_g, ln2_b):
    batch, seq, d = x.shape
    depth = w_in.shape[0]
    t = batch * seq
    alpha = (2 * depth) ** 0.25
    rope = _rope_tables(seq)
    tm = _tile(seq, 512)

    sizes = (A_Q, A_KV, A_KV, SSD_INNER, SSD_INNER + SSD_BC_WIDTH, SSD_HEADS, C_QKV, C_QKV, C_QKV, 3 * d)
    offs = [0]
    for s in sizes:
        offs.append(offs[-1] + s)
    o_qa, o_ka, o_va, o_z, o_xbc, o_dt, o_qc, o_kc, o_vc, o_gate, o_end = offs

    n_a, n_misc, n_gate = o_z - o_qa, o_qc - o_z, o_end - o_gate
    dt_pad = LANES - SSD_HEADS

    def piece(lo, hi, pad=0):
        wp, bp = w_in[:, :, lo:hi], b_in[:, None, lo:hi]
        if pad:
            wp, bp = jnp.pad(wp, ((0, 0), (0, 0), (0, pad))), jnp.pad(bp, ((0, 0), (0, 0), (0, pad)))
        return wp.astype(BF16), bp

    w_a, b_a = piece(o_qa, o_z)
    w_misc, b_misc = piece(o_z, o_qc, dt_pad)
    w_c, b_c = piece(o_qc, o_gate)
    w_g, b_g = piece(o_gate, o_end)
    p_swa, p_ssd, p_dil, w_o = (m.astype(BF16) for m in (proj_swa, proj_ssd, proj_dil, w_out))
    wr_all = jnp.concatenate(
        [router_group_w, router_expert_w.transpose(0, 2, 1, 3).reshape(depth, d, N_EXPERTS)], axis=2)
    wr_all = jnp.pad(wr_all, ((0, 0), (0, 0), (0, LANES - N_GROUPS - N_EXPERTS))).astype(BF16)
    br_all = jnp.pad(jnp.concatenate([router_group_b, router_expert_b.reshape(depth, N_EXPERTS)], axis=1),
                     ((0, 0), (0, LANES - N_GROUPS - N_EXPERTS)))

    xf = x.reshape(t, d)
    xb = xf.astype(BF16)
    for l in range(depth):
        qkv_a = _proj(xb, w_a, b_a, l, 0, n_a, BF16, tm, n_a, seq, rope=rope, rope_blocks=(A_Q + A_KV) // LANES,
                      dup_from=A_Q // LANES, name="proj_swa_qkv")
        qk_c = _proj(xb, w_c, b_c, l, 0, 2 * C_QKV, F32, tm, C_QKV, seq, rope=rope, rope_blocks=C_QKV // LANES, name="proj_dil_qk")
        v_c = _proj(xb, w_c, b_c, l, 2 * C_QKV, C_QKV, F32, tm, C_QKV, seq, name="proj_dil_v")
        misc = _proj(xb, w_misc, b_misc, l, 0, n_misc + dt_pad, F32, tm, (n_misc + dt_pad) // 3, seq, name="proj_ssd")
        gates = _proj(xb, w_g, b_g, l, 0, n_gate, F32, tm, 1024, seq, act="sigmoid", name="proj_gates")

        y_swa = _swa_attention(qkv_a, attn_sinks[l], batch, seq)
        y_ssd = _ssd_mixer(misc, conv_w[l], conv_b[l], dt_bias[l], a_log[l], d_skip[l], ssd_norm_w[l], batch, seq)
        dil = [_dilated_attention(qk_c, v_c, batch, seq, gi) for gi in range(len(DIL_CONFIGS))]

        merged = _merge(y_swa, y_ssd, dil, gates, p_swa[l], p_ssd[l], p_dil[l], tm, 1024)
        x1, ids, comb, cnt = _ln1_router(xf, merged, w_o[l], ln1_g[l], ln1_b[l], wr_all[l], br_all[l].reshape(1, LANES), alpha, tm)

        plan = _dispatch_plan(ids, cnt[0, N_GROUPS:N_GROUPS + N_EXPERTS].astype(I32), t)
        y2 = _moe_experts(x1, plan, expert_w1, expert_w3, expert_w2, l)
        xf, xb = _ln2(x1, y2, comb, ln2_g[l], ln2_b[l], alpha, tm)
    return xf.reshape(batch, seq, d)
```

```python
import functools

import jax
import jax.numpy as jnp
from jax import lax
from jax.experimental import pallas as pl
from jax.experimental.pallas import tpu as pltpu

F32 = jnp.float32
BF16 = jnp.bfloat16
I32 = jnp.int32

LANES = 128
SUBLANES = 8
VMEM_LIMIT_BYTES = 48 * 1024 * 1024

HEAD_DIM = 64
HALF_HEAD = HEAD_DIM // 2
ROPE_THETA = 10000.0
ATTN_BLOCK = 128
LN_EPS = 1e-5

SWA_Q_HEADS = 16
SWA_KV_HEADS = 2
SWA_WINDOW = 128

SSD_HEADS = 24
SSD_HEADDIM = 64
SSD_INNER = SSD_HEADS * SSD_HEADDIM
SSD_GROUPS = 4
SSD_STATE = 128
SSD_CONV = 4
SSD_CHUNK = 128
SSD_HEADS_PER_GROUP = SSD_HEADS // SSD_GROUPS
SSD_GROUP_WIDTH = SSD_INNER // SSD_GROUPS
SSD_BC_WIDTH = 2 * SSD_GROUPS * SSD_STATE

DIL_CONFIGS = ((128, 1), (512, 4), (2048, 16))
DIL_HEADS_PER_GROUP = 6
DIL_HEADS = DIL_HEADS_PER_GROUP * len(DIL_CONFIGS)

A_Q = SWA_Q_HEADS * HEAD_DIM
A_KV = SWA_KV_HEADS * HEAD_DIM
A_KV_DUP = 2 * A_KV
C_QKV = DIL_HEADS * HEAD_DIM
C_OUT = DIL_HEADS_PER_GROUP * HEAD_DIM

N_GROUPS = 4
EXPERTS_PER_GROUP = 8
N_EXPERTS = N_GROUPS * EXPERTS_PER_GROUP
TOP_K = 2
D_EXPERT = 512
MOE_BLOCK = 128
MOE_DMA_UNROLL = 8

NEG_INF = float("-inf")


def _cparams(*sem):
    return pltpu.CompilerParams(dimension_semantics=sem, vmem_limit_bytes=VMEM_LIMIT_BYTES)


def _sigmoid(v):
    return 1.0 / (1.0 + jnp.exp(-v))


def _proj_kernel(*refs, rope_blocks, dup_from, act):
    if rope_blocks:
        x_ref, w_ref, b_ref, cos_ref, sin_ref, o_ref = refs
    else:
        x_ref, w_ref, b_ref, o_ref = refs
    acc = jnp.dot(x_ref[...], w_ref[...], preferred_element_type=F32) + b_ref[...]
    n_blocks = acc.shape[1] // LANES
    if rope_blocks or dup_from < n_blocks:
        lane = lax.broadcasted_iota(I32, (acc.shape[0], LANES), 1)
        first_half = (lane % HEAD_DIM) < HALF_HEAD
        low_head = lane < HEAD_DIM
        out_c = 0
        for c in range(n_blocks):
            v = acc[:, c * LANES:(c + 1) * LANES]
            if c < rope_blocks:
                partner = jnp.where(first_half, pltpu.roll(v, LANES - HALF_HEAD, 1), pltpu.roll(v, HALF_HEAD, 1))
                v = v * cos_ref[...] + partner * sin_ref[...]
            if c >= dup_from:
                other = pltpu.roll(v, HEAD_DIM, 1)
                outs = (jnp.where(low_head, v, other), jnp.where(low_head, other, v))
            else:
                outs = (v,)
            for o in outs:
                o_ref[:, out_c * LANES:(out_c + 1) * LANES] = o.astype(o_ref.dtype)
                out_c += 1
    elif act == "sigmoid":
        o_ref[...] = _sigmoid(acc).astype(o_ref.dtype)
    else:
        o_ref[...] = acc.astype(o_ref.dtype)


def _proj(xb, w, b, layer, col0, n, out_dtype, tm, tn, seq, rope=None, rope_blocks=0, dup_from=None, act=None, name="proj"):
    t, d = xb.shape
    assert t % tm == 0 and n % tn == 0 and seq % tm == 0 and col0 % tn == 0
    j0 = col0 // tn
    in_blocks = tn // LANES
    if dup_from is None:
        dup_from = in_blocks
    assert dup_from == in_blocks or n == tn
    tn_out = tn + (in_blocks - dup_from) * LANES
    in_specs = [
        pl.BlockSpec((tm, d), lambda j, i: (i, 0)),
        pl.BlockSpec((None, d, tn), lambda j, i: (layer, 0, j0 + j)),
        pl.BlockSpec((None, 1, tn), lambda j, i: (layer, 0, j0 + j)),
    ]
    args = [xb, w, b]
    if rope_blocks:
        sblocks = seq // tm
        in_specs += [pl.BlockSpec((tm, LANES), lambda j, i: (i % sblocks, 0))] * 2
        args += list(rope)
    return pl.pallas_call(
        functools.partial(_proj_kernel, rope_blocks=rope_blocks, dup_from=dup_from, act=act),
        grid=(n // tn, t // tm),
        in_specs=in_specs,
        out_specs=pl.BlockSpec((tm, tn_out), lambda j, i: (i, j)),
        out_shape=jax.ShapeDtypeStruct((t, n // tn * tn_out), out_dtype),
        compiler_params=_cparams("arbitrary", "arbitrary"),
        name=name,
    )(*args)


def _band_biases(max_dist):
    row = lax.broadcasted_iota(I32, (ATTN_BLOCK, 2 * ATTN_BLOCK), 0)
    col = lax.broadcasted_iota(I32, (ATTN_BLOCK, 2 * ATTN_BLOCK), 1)
    dist = row + ATTN_BLOCK - col
    in_band = (dist >= 0) & (dist <= max_dist)
    bias_any = jnp.where(in_band, 0.0, NEG_INF).astype(F32)
    bias_first = jnp.where(in_band & (col >= ATTN_BLOCK), 0.0, NEG_INF).astype(F32)
    return bias_any, bias_first


def _attend_pair(q2, k2, v2, bias, sinks):
    lane = lax.broadcasted_iota(I32, (ATTN_BLOCK, LANES), 1)
    low = lane < HEAD_DIM
    scale = HEAD_DIM ** -0.5
    outs, lses = [], []
    for half in range(2):
        qh = jnp.where(low if half == 0 else jnp.logical_not(low), q2, jnp.zeros_like(q2))
        s = lax.dot_general(qh, k2, (((1,), (1,)), ((), ())), preferred_element_type=F32) * scale
        s = s + bias
        m = jnp.max(s, axis=-1, keepdims=True)
        if sinks is not None:
            m = jnp.maximum(m, sinks[half])
        p = jnp.exp(s - m)
        den = jnp.sum(p, axis=-1, keepdims=True)
        if sinks is not None:
            den = den + jnp.exp(sinks[half] - m)
        outs.append(jnp.dot(p.astype(BF16), v2, preferred_element_type=F32) / den)
        lses.append(m + jnp.log(den))
    return jnp.where(low, outs[0], outs[1]), jnp.where(low, lses[0], lses[1])


def _swa_attn_kernel(sink_ref, q_ref, kc_ref, kp_ref, vc_ref, vp_ref, o_ref, kf, vf, *, pairs_per_kv, tq):
    n = pl.program_id(1)
    kf[0:ATTN_BLOCK, :] = kp_ref[...]
    kf[ATTN_BLOCK:, :] = kc_ref[...]
    vf[0:ATTN_BLOCK, :] = vp_ref[...]
    vf[ATTN_BLOCK:, :] = vc_ref[...]
    bias_any, bias_first = _band_biases(SWA_WINDOW - 1)

    def sub_block(j, carry):
        r0 = pl.multiple_of(j * ATTN_BLOCK, ATTN_BLOCK)
        bias = jnp.where(jnp.logical_and(n == 0, j == 0), bias_first, bias_any)
        for hp in range(SWA_Q_HEADS // 2):
            cols = slice(hp * LANES, (hp + 1) * LANES)
            kv_cols = slice((hp // pairs_per_kv) * LANES, (hp // pairs_per_kv + 1) * LANES)
            o, _ = _attend_pair(q_ref[pl.ds(r0, ATTN_BLOCK), cols], kf[pl.ds(r0, 2 * ATTN_BLOCK), kv_cols],
                                vf[pl.ds(r0, 2 * ATTN_BLOCK), kv_cols], bias, (sink_ref[2 * hp], sink_ref[2 * hp + 1]))
            o_ref[pl.ds(r0, ATTN_BLOCK), cols] = o.astype(o_ref.dtype)
        return carry

    lax.fori_loop(0, tq // ATTN_BLOCK, sub_block, 0)


def _dil_attn_kernel(q_ref, kc_ref, kp_ref, vc_ref, vp_ref, o_ref, l_ref, *, dil, max_dist):
    n = pl.program_id(1)
    rows = q_ref.shape[0]
    sub_rows = rows // dil
    bias_any, bias_first = _band_biases(max_dist)
    bias_head = jnp.where(n == 0, bias_first, bias_any)

    def rows_of(ref, r, start, count):
        if dil == 1:
            return ref[start:start + count, :]
        return ref[pl.ds(start * dil + r, count, stride=dil), :]

    for r in range(dil):
        k_prev = rows_of(kp_ref, r, 0, ATTN_BLOCK).astype(BF16)
        v_prev = rows_of(vp_ref, r, 0, ATTN_BLOCK).astype(BF16)
        for j in range(sub_rows // ATTN_BLOCK):
            start = j * ATTN_BLOCK
            k_cur = rows_of(kc_ref, r, start, ATTN_BLOCK).astype(BF16)
            v_cur = rows_of(vc_ref, r, start, ATTN_BLOCK).astype(BF16)
            q2 = rows_of(q_ref, r, start, ATTN_BLOCK).astype(BF16)
            o, l = _attend_pair(q2, jnp.concatenate([k_prev, k_cur], axis=0), jnp.concatenate([v_prev, v_cur], axis=0),
                                bias_head if j == 0 else bias_any, None)
            if dil == 1:
                o_ref[start:start + ATTN_BLOCK, :] = o
                l_ref[start:start + ATTN_BLOCK, :] = l
            else:
                o_ref[pl.ds(start * dil + r, ATTN_BLOCK, stride=dil), :] = o
                l_ref[pl.ds(start * dil + r, ATTN_BLOCK, stride=dil), :] = l
            k_prev, v_prev = k_cur, v_cur


def _attn_tile(length):
    return min(512, length)


def _swa_attention(qkv, sinks, batch, seq):
    tq = _attn_tile(seq)
    qkv = qkv.reshape(batch, seq, A_Q + 2 * A_KV_DUP)
    k_blk = A_Q // A_KV_DUP
    v_blk = k_blk + 1
    sub = tq // ATTN_BLOCK

    def prev(n):
        return jnp.maximum(n * sub - 1, 0)

    kern = functools.partial(_swa_attn_kernel, pairs_per_kv=SWA_Q_HEADS // SWA_KV_HEADS // 2, tq=tq)
    return pl.pallas_call(
        kern,
        grid=(batch, seq // tq),
        in_specs=[
            pl.BlockSpec(memory_space=pltpu.SMEM),
            pl.BlockSpec((None, tq, A_Q), lambda b, n: (b, n, 0)),
            pl.BlockSpec((None, tq, A_KV_DUP), lambda b, n: (b, n, k_blk)),
            pl.BlockSpec((None, ATTN_BLOCK, A_KV_DUP), lambda b, n: (b, prev(n), k_blk)),
            pl.BlockSpec((None, tq, A_KV_DUP), lambda b, n: (b, n, v_blk)),
            pl.BlockSpec((None, ATTN_BLOCK, A_KV_DUP), lambda b, n: (b, prev(n), v_blk)),
        ],
        out_specs=pl.BlockSpec((None, tq, A_Q), lambda b, n: (b, n, 0)),
        out_shape=jax.ShapeDtypeStruct((batch, seq, A_Q), BF16),
        scratch_shapes=[pltpu.VMEM((tq + ATTN_BLOCK, A_KV_DUP), BF16), pltpu.VMEM((tq + ATTN_BLOCK, A_KV_DUP), BF16)],
        compiler_params=_cparams("arbitrary", "arbitrary"),
        name="swa_attn",
    )(sinks, qkv, qkv, qkv, qkv, qkv).reshape(batch * seq, A_Q)


def _dilated_attention(qk, v, batch, seq, gi):
    window, dil = DIL_CONFIGS[gi]
    back = ATTN_BLOCK * dil
    rows = max(_attn_tile(seq), back)
    assert seq % rows == 0 and rows % back == 0
    pairs = DIL_HEADS_PER_GROUP // 2
    q_blk, k_blk, v_blk = gi * pairs, (C_QKV // LANES) + gi * pairs, gi * pairs
    per_seq = seq // rows
    ratio = rows // back

    def cur(col0):
        return pl.BlockSpec((rows, LANES), lambda b, n, hp: (b * per_seq + n, col0 + hp))

    def prev(col0):
        return pl.BlockSpec((back, LANES), lambda b, n, hp: (b * per_seq * ratio + jnp.maximum(n * ratio - 1, 0), col0 + hp))

    out_sds = jax.ShapeDtypeStruct((batch * seq, C_OUT), F32)
    out_spec = pl.BlockSpec((rows, LANES), lambda b, n, hp: (b * per_seq + n, hp))
    return pl.pallas_call(
        functools.partial(_dil_attn_kernel, dil=dil, max_dist=window // dil),
        grid=(batch, per_seq, pairs),
        in_specs=[cur(q_blk), cur(k_blk), prev(k_blk), cur(v_blk), prev(v_blk)],
        out_specs=[out_spec, out_spec],
        out_shape=[out_sds, out_sds],
        compiler_params=_cparams("arbitrary", "arbitrary", "arbitrary"),
        name=f"dil_attn_{dil}",
    )(qk, qk, qk, v, v)


def _ssd_kernel(z_ref, xs_ref, bc_ref, dt_ref, cwx_ref, cwbc_ref, cbx_ref, cbbc_ref, dtb_ref, alog_ref,
                dsk_ref, nw_ref, o_ref, extx, extbc, hst):
    q = SSD_CHUNK
    c = pl.program_id(1)

    @pl.when(c == 0)
    def _():
        extx[0:SUBLANES, :] = jnp.zeros((SUBLANES, SSD_INNER), F32)
        extbc[0:SUBLANES, :] = jnp.zeros((SUBLANES, SSD_BC_WIDTH), F32)
        hst[...] = jnp.zeros(hst.shape, F32)

    extx[SUBLANES:, :] = xs_ref[...]
    extbc[SUBLANES:, :] = bc_ref[...]

    def conv_silu(ext, cw_ref, cb_ref):
        acc = cb_ref[...]
        for k in range(SSD_CONV):
            start = SUBLANES - (SSD_CONV - 1) + k
            acc = acc + cw_ref[k:k + 1, :] * ext[start:start + q, :]
        return acc * _sigmoid(acc)

    xa = conv_silu(extx, cwx_ref, cbx_ref)
    bca = conv_silu(extbc, cwbc_ref, cbbc_ref)
    extx[0:SUBLANES, :] = xs_ref[q - SUBLANES:q, :]
    extbc[0:SUBLANES, :] = bc_ref[q - SUBLANES:q, :]

    dtv = dt_ref[...] + dtb_ref[...]
    dt = jnp.maximum(dtv, 0.0) + jnp.log(1.0 + jnp.exp(-jnp.abs(dtv)))
    a = -jnp.exp(alog_ref[...])
    rr = lax.broadcasted_iota(I32, (q, q), 0)
    cc = lax.broadcasted_iota(I32, (q, q), 1)
    causal = rr >= cc
    tril = jnp.where(causal, 1.0, 0.0).astype(F32)
    acs = jnp.dot(tril, dt * a, preferred_element_type=F32, precision=lax.Precision.HIGHEST)
    acs_t = acs.T

    low = lax.broadcasted_iota(I32, (q, LANES), 1) < SSD_HEADDIM
    low_state = lax.broadcasted_iota(I32, (SSD_STATE, LANES), 1) < SSD_HEADDIM
    pairs_per_group = SSD_HEADS_PER_GROUP // 2

    def per_head(h0, fn):
        return jnp.where(low, fn(h0), fn(h0 + 1))

    for g in range(SSD_GROUPS):
        bg = bca[:, g * SSD_STATE:(g + 1) * SSD_STATE]
        cg = bca[:, (SSD_GROUPS + g) * SSD_STATE:(SSD_GROUPS + g + 1) * SSD_STATE].astype(BF16)
        bt = bg.T.astype(BF16)
        cb = jnp.dot(cg, bt, preferred_element_type=F32)
        ys = []
        for pi in range(pairs_per_group):
            pair = g * pairs_per_group + pi
            h0 = 2 * pair
            cols = slice(pair * LANES, (pair + 1) * LANES)
            xp = xa[:, cols]
            xdt = xp * per_head(h0, lambda h: dt[:, h:h + 1])
            m_both = jnp.concatenate(
                [(cb * jnp.exp(jnp.where(causal, acs[:, h:h + 1] - acs_t[h:h + 1, :], NEG_INF))).astype(BF16)
                 for h in (h0, h0 + 1)], axis=1)
            x_both = jnp.concatenate([jnp.where(low, xdt, 0.0), jnp.where(low, 0.0, xdt)], axis=0).astype(BF16)
            y = jnp.dot(m_both, x_both, preferred_element_type=F32)
            hprev = hst[pair]
            y = y + jnp.dot(cg, hprev.astype(BF16), preferred_element_type=F32) * per_head(h0, lambda h: jnp.exp(acs[:, h:h + 1]))
            y = y + xp * dsk_ref[:, cols]
            to_end = per_head(h0, lambda h: jnp.exp(acs[q - 1:q, h:h + 1] - acs[:, h:h + 1]))
            chunk_decay = jnp.where(low_state, jnp.exp(acs[q - 1:q, h0:h0 + 1]), jnp.exp(acs[q - 1:q, h0 + 1:h0 + 2]))
            hst[pair] = hprev * chunk_decay + jnp.dot(bt, (xdt * to_end).astype(BF16), preferred_element_type=F32)
            ys.append(y)
        lo, hi = g * SSD_GROUP_WIDTH, (g + 1) * SSD_GROUP_WIDTH
        zg = z_ref[:, lo:hi]
        yg = jnp.concatenate(ys, axis=1) * (zg * _sigmoid(zg))
        yg = yg * lax.rsqrt(jnp.mean(yg * yg, axis=-1, keepdims=True) + LN_EPS)
        o_ref[:, lo:hi] = (yg * nw_ref[:, lo:hi]).astype(o_ref.dtype)


def _ssd_mixer(misc, conv_w, conv_b, dt_bias, a_log, d_skip, norm_w, batch, seq):
    q = SSD_CHUNK
    nc = seq // q
    pad = LANES - SSD_HEADS
    row = lambda v: v.reshape(1, -1)
    args = [
        misc, misc, misc, misc,
        conv_w[:, :SSD_INNER], conv_w[:, SSD_INNER:], row(conv_b[:SSD_INNER]), row(conv_b[SSD_INNER:]),
        row(jnp.pad(dt_bias, (0, pad))), row(jnp.pad(a_log, (0, pad))),
        row(jnp.repeat(d_skip, SSD_HEADDIM)), row(norm_w),
    ]
    tok = lambda b, c: b * nc + c
    full = lambda shape: pl.BlockSpec(shape, lambda b, c: (0, 0))
    bc_blk = 2 * SSD_INNER // SSD_BC_WIDTH
    dt_blk = (2 * SSD_INNER + SSD_BC_WIDTH) // LANES
    return pl.pallas_call(
        _ssd_kernel,
        grid=(batch, nc),
        in_specs=[
            pl.BlockSpec((q, SSD_INNER), lambda b, c: (tok(b, c), 0)),
            pl.BlockSpec((q, SSD_INNER), lambda b, c: (tok(b, c), 1)),
            pl.BlockSpec((q, SSD_BC_WIDTH), lambda b, c: (tok(b, c), bc_blk)),
            pl.BlockSpec((q, LANES), lambda b, c: (tok(b, c), dt_blk)),
            full((SSD_CONV, SSD_INNER)), full((SSD_CONV, SSD_BC_WIDTH)), full((1, SSD_INNER)), full((1, SSD_BC_WIDTH)),
            full((1, LANES)), full((1, LANES)), full((1, SSD_INNER)), full((1, SSD_INNER)),
        ],
        out_specs=pl.BlockSpec((q, SSD_INNER), lambda b, c: (tok(b, c), 0)),
        out_shape=jax.ShapeDtypeStruct((batch * seq, SSD_INNER), BF16),
        scratch_shapes=[
            pltpu.VMEM((SUBLANES + q, SSD_INNER), F32),
            pltpu.VMEM((SUBLANES + q, SSD_BC_WIDTH), F32),
            pltpu.VMEM((SSD_HEADS // 2, SSD_STATE, LANES), F32),
        ],
        compiler_params=_cparams("arbitrary", "arbitrary"),
        name="ssd_mixer",
    )(*args)


def _merge_kernel(ysw_ref, yss_ref, o0_ref, l0_ref, o1_ref, l1_ref, o2_ref, l2_ref, g0_ref, g1_ref, g2_ref,
                  p0_ref, p1_ref, p2_ref, out_ref):
    l0, l1, l2 = l0_ref[...], l1_ref[...], l2_ref[...]
    m = jnp.maximum(jnp.maximum(l0, l1), l2)
    e0, e1, e2 = jnp.exp(l0 - m), jnp.exp(l1 - m), jnp.exp(l2 - m)
    inv = 1.0 / (e0 + e1 + e2)
    ydil = (o0_ref[...] * (e0 * inv) + o1_ref[...] * (e1 * inv) + o2_ref[...] * (e2 * inv)).astype(BF16)
    merged = g0_ref[...] * jnp.dot(ysw_ref[...], p0_ref[...], preferred_element_type=F32)
    merged = merged + g1_ref[...] * jnp.dot(yss_ref[...], p1_ref[...], preferred_element_type=F32)
    merged = merged + g2_ref[...] * jnp.dot(ydil, p2_ref[...], preferred_element_type=F32)
    out_ref[...] = merged.astype(out_ref.dtype)


def _merge(y_swa, y_ssd, dil, gates, p_swa, p_ssd, p_dil, tm, tn):
    t = y_swa.shape[0]
    d = p_swa.shape[1]
    nj = d // tn
    rows = lambda width: pl.BlockSpec((tm, width), lambda i, j: (i, 0))
    gate = lambda k: pl.BlockSpec((tm, tn), lambda i, j: (i, k * nj + j))
    wcol = lambda width: pl.BlockSpec((width, tn), lambda i, j: (0, j))
    dil_args = [a for pair in dil for a in pair]
    return pl.pallas_call(
        _merge_kernel,
        grid=(t // tm, nj),
        in_specs=[rows(A_Q), rows(SSD_INNER)] + [rows(C_OUT)] * 6 + [gate(0), gate(1), gate(2)]
        + [wcol(A_Q), wcol(SSD_INNER), wcol(C_OUT)],
        out_specs=pl.BlockSpec((tm, tn), lambda i, j: (i, j)),
        out_shape=jax.ShapeDtypeStruct((t, d), BF16),
        compiler_params=_cparams("arbitrary", "arbitrary"),
        name="merge",
    )(y_swa, y_ssd, *dil_args, gates, gates, gates, p_swa, p_ssd, p_dil)


def _layer_norm(h, g, b):
    mu = jnp.mean(h, axis=-1, keepdims=True)
    hc = h - mu
    var = jnp.mean(hc * hc, axis=-1, keepdims=True)
    return hc * lax.rsqrt(var + LN_EPS) * g + b


LN1_ROW_SPLIT = 2


def _ln1_router_kernel(x_ref, m_ref, w_ref, g_ref, b_ref, wr_ref, br_ref, xo_ref, ids_ref, comb_ref, cnt_ref, *, alpha):
    i = pl.program_id(0)
    tm = x_ref.shape[0]
    rows = tm // LN1_ROW_SPLIT
    lane = lax.broadcasted_iota(I32, (rows, LANES), 1).astype(F32)
    big = float(2 * LANES)

    def first_argmax(vals):
        top = jnp.max(vals, axis=-1, keepdims=True)
        return top, jnp.min(jnp.where(vals == top, lane, big), axis=-1, keepdims=True)

    tops = []
    for part in range(LN1_ROW_SPLIT):
        sl = slice(part * rows, (part + 1) * rows)
        h = alpha * x_ref[sl, :] + jnp.dot(m_ref[sl, :], w_ref[...], preferred_element_type=F32)
        y = _layer_norm(h, g_ref[...], b_ref[...])
        xo_ref[sl, :] = y

        logits = jnp.dot(y.astype(BF16), wr_ref[...], preferred_element_type=F32) + br_ref[...]
        gl = jnp.where(lane < N_GROUPS, logits, NEG_INF)
        gmax, gsel = first_argmax(gl)
        p_group = 1.0 / jnp.sum(jnp.exp(gl - gmax), axis=-1, keepdims=True)
        lo = N_GROUPS + gsel * EXPERTS_PER_GROUP
        el = jnp.where((lane >= lo) & (lane < lo + EXPERTS_PER_GROUP), logits, NEG_INF)
        v1, i1 = first_argmax(el)
        v2, i2 = first_argmax(jnp.where(lane == i1, NEG_INF, el))
        e21 = jnp.exp(v2 - v1)
        inv = 1.0 / (1.0 + e21)
        comb_ref[sl, :] = jnp.where(lane == 0.0, inv * p_group, jnp.where(lane == 1.0, e21 * inv * p_group, 0.0))
        tops.append((i1, i2))

    @pl.when(i == 0)
    def _():
        cnt_ref[...] = jnp.zeros(cnt_ref.shape, F32)

    i1 = jnp.concatenate([t[0] for t in tops], axis=0)
    i2 = jnp.concatenate([t[1] for t in tops], axis=0)
    lane = lax.broadcasted_iota(I32, (tm, LANES), 1).astype(F32)
    pick1 = jnp.where(lane == i1, 1.0, 0.0)
    pick2 = jnp.where(lane == i2, 1.0, 0.0)
    picks = pick1 + pick2
    earlier = lax.broadcasted_iota(I32, (tm, tm), 0) > lax.broadcasted_iota(I32, (tm, tm), 1)
    before = jnp.dot(jnp.where(earlier, 1.0, 0.0).astype(BF16), picks.astype(BF16), preferred_element_type=F32) + cnt_ref[...]
    rank1 = jnp.sum(pick1 * before, axis=-1, keepdims=True)
    rank2 = jnp.sum(pick2 * before, axis=-1, keepdims=True)
    cnt_ref[...] = cnt_ref[...] + jnp.sum(picks, axis=0, keepdims=True)
    ids = jnp.where(lane == 0.0, i1 - N_GROUPS, jnp.where(lane == 1.0, i2 - N_GROUPS,
                    jnp.where(lane == 2.0, rank1, jnp.where(lane == 3.0, rank2, 0.0))))
    for c in range(tm // LANES):
        ids_ref[:, c * LANES:(c + 1) * LANES] = ids[c * LANES:(c + 1) * LANES, :].T[0:SUBLANES, :].astype(I32)


def _ln1_router(x, merged, w_out, g, b, wr, br, alpha, tm):
    t, d = x.shape
    rows = lambda width: pl.BlockSpec((tm, width), lambda i: (i, 0))
    full = lambda shape: pl.BlockSpec(shape, lambda i: (0, 0))
    once = lambda shape: pl.BlockSpec(shape, lambda i: (0, 0), pipeline_mode=pl.Buffered(1))
    return pl.pallas_call(
        functools.partial(_ln1_router_kernel, alpha=alpha),
        grid=(t // tm,),
        in_specs=[rows(d), rows(d), once((d, d)), full((1, d)), full((1, d)), once((d, LANES)), full((1, LANES))],
        out_specs=[rows(d), pl.BlockSpec((SUBLANES, tm), lambda i: (0, i)), rows(LANES), full((1, LANES))],
        out_shape=[jax.ShapeDtypeStruct((t, d), F32),
                   jax.ShapeDtypeStruct((SUBLANES, t), I32), jax.ShapeDtypeStruct((t, LANES), F32),
                   jax.ShapeDtypeStruct((1, LANES), F32)],
        compiler_params=_cparams("arbitrary"),
        name="ln1_router",
    )(x, merged, w_out, g.reshape(1, d), b.reshape(1, d), wr, br)


def _moe_kernel(be_ref, dest_ref, pad_lo_ref, pad_hi_ref, nu_ref, x_hbm, w1_ref, w3_ref, w2_ref, y_hbm,
                xbuf0, xbuf1, ybuf0, ybuf1, w1b, w3b, w2b, tok_ref, dst_ref, gsem, ssem, *, n_tok):
    b = pl.program_id(0)
    n_used = nu_ref[0]
    xbufs, ybufs = (xbuf0, xbuf1), (ybuf0, ybuf1)
    trash_row0 = TOP_K * n_tok

    def entry(blk, r):
        return (blk + 1) * MOE_BLOCK + r

    def gather_copy(blk, p, r):
        return pltpu.make_async_copy(
            x_hbm.at[pl.ds(tok_ref[entry(blk, r)], 1), :], xbufs[p].at[pl.ds(r, 1), :], gsem.at[p])

    def scatter_copy(blk, p, r):
        return pltpu.make_async_copy(
            ybufs[p].at[pl.ds(r, 1), :], y_hbm.at[pl.ds(dst_ref[entry(blk, r)], 1), :], ssem.at[p])

    def for_rows(fn):
        def chunk(c, carry):
            for u in range(MOE_DMA_UNROLL):
                fn(c * MOE_DMA_UNROLL + u)
            return carry
        lax.fori_loop(0, MOE_BLOCK // MOE_DMA_UNROLL, chunk, 0)

    @pl.when(b == 0)
    def _():
        def pad_entries(lo, hi):
            def pad_entry(s, carry):
                tok_ref[s] = 0
                dst_ref[s] = trash_row0 + lax.bitwise_and(s, MOE_BLOCK - 1)
                return carry
            lax.fori_loop(lo, hi, pad_entry, 0)

        pad_entries(0, MOE_BLOCK)
        lax.fori_loop(0, N_EXPERTS, lambda e, c: (pad_entries(entry(0, pad_lo_ref[e]), entry(0, pad_hi_ref[e])), c)[1], 0)
        pad_entries(entry(n_used, 0), entry(n_used + 1, 0))

        for k in range(TOP_K):
            def fill(c, carry):
                for u in range(MOE_DMA_UNROLL):
                    tkn = c * MOE_DMA_UNROLL + u
                    s = dest_ref[k * n_tok + tkn] + MOE_BLOCK
                    tok_ref[s] = tkn
                    dst_ref[s] = k * n_tok + tkn
                return carry
            lax.fori_loop(0, n_tok // MOE_DMA_UNROLL, fill, 0)

        ybuf1[...] = jnp.zeros(ybuf1.shape, F32)
        for_rows(lambda r: gather_copy(0, 0, r).start())

    def step(p):
        q = 1 - p
        for_rows(lambda r: gather_copy(b, p, r).wait())

        @pl.when(b >= 1)
        def _():
            for_rows(lambda r: scatter_copy(b - 2, p, r).wait())

        @pl.when(jnp.logical_or(b == 0, be_ref[b] != be_ref[jnp.maximum(b - 1, 0)]))
        def _():
            w1b[...] = w1_ref[...].astype(BF16)
            w3b[...] = w3_ref[...].astype(BF16)
            w2b[...] = w2_ref[...].astype(BF16)

        def issue(part):
            def row(r, carry):
                gather_copy(b + 1, q, r).start()
                scatter_copy(b - 1, q, r).start()
                return carry
            lax.fori_loop(part * MOE_BLOCK // 4, (part + 1) * MOE_BLOCK // 4, row, 0, unroll=True)

        xb = xbufs[p][...].astype(BF16)
        issue(0)
        h1 = jnp.dot(xb, w1b[...], preferred_element_type=F32)
        issue(1)
        h3 = jnp.dot(xb, w3b[...], preferred_element_type=F32)
        issue(2)
        h = (h1 * _sigmoid(h1) * h3).astype(BF16)
        issue(3)
        ybufs[p][...] = jnp.dot(h, w2b[...], preferred_element_type=F32)

        @pl.when(b == n_used - 1)
        def _():
            for_rows(lambda r: scatter_copy(b - 1, q, r).wait())
            for_rows(lambda r: scatter_copy(b, p, r).start())
            for_rows(lambda r: scatter_copy(b, p, r).wait())
            for_rows(lambda r: gather_copy(b + 1, q, r).wait())

    for p in range(2):
        pl.when(jnp.logical_and(b < n_used, b % 2 == p))(functools.partial(step, p))


def _moe_experts(x1, plan, w1, w3, w2, layer):
    t, d = x1.shape
    block_e, dest, pad_lo, pad_hi, n_used = plan
    n_blocks = block_e.shape[0]
    table = pltpu.SMEM(((n_blocks + 2) * MOE_BLOCK,), I32)
    wspec = lambda shape: pl.BlockSpec((None, None) + shape, lambda b, be, *_: (layer, be[b], 0, 0))
    grid_spec = pltpu.PrefetchScalarGridSpec(
        num_scalar_prefetch=5,
        grid=(n_blocks,),
        in_specs=[pl.BlockSpec(memory_space=pl.ANY), wspec((d, D_EXPERT)), wspec((d, D_EXPERT)), wspec((D_EXPERT, d))],
        out_specs=pl.BlockSpec(memory_space=pl.ANY),
        scratch_shapes=[
            pltpu.VMEM((MOE_BLOCK, d), F32), pltpu.VMEM((MOE_BLOCK, d), F32),
            pltpu.VMEM((MOE_BLOCK, d), F32), pltpu.VMEM((MOE_BLOCK, d), F32),
            pltpu.VMEM((d, D_EXPERT), BF16), pltpu.VMEM((d, D_EXPERT), BF16), pltpu.VMEM((D_EXPERT, d), BF16),
            table, table,
            pltpu.SemaphoreType.DMA((2,)),
            pltpu.SemaphoreType.DMA((2,)),
        ],
    )
    return pl.pallas_call(
        functools.partial(_moe_kernel, n_tok=t),
        grid_spec=grid_spec,
        out_shape=jax.ShapeDtypeStruct((TOP_K * t + MOE_BLOCK, d), F32),
        compiler_params=_cparams("arbitrary"),
        name="moe_experts",
    )(block_e, dest, pad_lo, pad_hi, n_used, x1, w1, w3, w2)


def _ln2_kernel(x_ref, y0_ref, y1_ref, comb_ref, g_ref, b_ref, xo_ref, xb_ref, *, alpha):
    comb = comb_ref[...]
    h = alpha * x_ref[...] + comb[:, 0:1] * y0_ref[...] + comb[:, 1:2] * y1_ref[...]
    y = _layer_norm(h, g_ref[...], b_ref[...])
    xo_ref[...] = y
    xb_ref[...] = y.astype(BF16)


def _ln2(x1, y2, comb, g, b, alpha, tm):
    t, d = x1.shape
    nblk = t // tm
    rows = lambda width: pl.BlockSpec((tm, width), lambda i: (i, 0))
    full = lambda shape: pl.BlockSpec(shape, lambda i: (0, 0))
    return pl.pallas_call(
        functools.partial(_ln2_kernel, alpha=alpha),
        grid=(nblk,),
        in_specs=[rows(d), rows(d), pl.BlockSpec((tm, d), lambda i: (nblk + i, 0)), rows(LANES), full((1, d)), full((1, d))],
        out_specs=[rows(d), rows(d)],
        out_shape=[jax.ShapeDtypeStruct((t, d), F32), jax.ShapeDtypeStruct((t, d), BF16)],
        compiler_params=_cparams("arbitrary"),
        name="ln2",
    )(x1, y2, y2, comb, g.reshape(1, d), b.reshape(1, d))


def _dispatch_plan(ids, counts, n_tok):
    n_assign = n_tok * TOP_K
    n_blocks = -(-n_assign // MOE_BLOCK) + N_EXPERTS
    experts = jnp.arange(N_EXPERTS, dtype=I32)
    padded = (counts + MOE_BLOCK - 1) // MOE_BLOCK * MOE_BLOCK
    pad_end = jnp.cumsum(padded)
    pad_start = pad_end - padded
    slot0 = jnp.sum(jnp.where(ids[0:TOP_K, :, None] == experts, pad_start, 0), axis=-1)
    dest = (slot0 + ids[TOP_K:2 * TOP_K]).reshape(n_assign)
    starts = jnp.arange(n_blocks, dtype=I32) * MOE_BLOCK
    block_e = jnp.minimum(jnp.sum((pad_end[None, :] <= starts[:, None]).astype(I32), axis=1), N_EXPERTS - 1)
    return block_e, dest, pad_start + counts, pad_end, pad_end[-1:] // MOE_BLOCK


def _rope_tables(seq):
    inv_freq = 1.0 / (ROPE_THETA ** (jnp.arange(0, HEAD_DIM, 2, dtype=F32) / HEAD_DIM))
    ang = jnp.arange(seq, dtype=F32)[:, None] * inv_freq[None, :]
    cos, sin = jnp.cos(ang), jnp.sin(ang)
    reps = LANES // HEAD_DIM
    return jnp.tile(jnp.concatenate([cos, cos], axis=1), (1, reps)), jnp.tile(jnp.concatenate([-sin, sin], axis=1), (1, reps))


def _tile(total, want):
    return min(total, want)


def kernel(x, w_in, b_in, attn_sinks, conv_w, conv_b, dt_bias, a_log, d_skip, ssd_norm_w, proj_swa, proj_ssd, proj_dil, w_out, ln1_g, ln1_b, router_group_w, router_group_b, router_expert_w, router_expert_b, expert_w1, expert_w3, expert_w2, ln2_g, ln2_b):
    batch, seq, d = x.shape
    depth = w_in.shape[0]
    t = batch * seq
    alpha = (2 * depth) ** 0.25
    rope = _rope_tables(seq)
    tm = _tile(seq, 512)
    tmp = _tile(seq, 1024)

    sizes = (A_Q, A_KV, A_KV, SSD_INNER, SSD_INNER + SSD_BC_WIDTH, SSD_HEADS, C_QKV, C_QKV, C_QKV, 3 * d)
    offs = [0]
    for s in sizes:
        offs.append(offs[-1] + s)
    o_qa, o_ka, o_va, o_z, o_xbc, o_dt, o_qc, o_kc, o_vc, o_gate, o_end = offs

    n_a, n_misc, n_gate = o_z - o_qa, o_qc - o_z, o_end - o_gate
    dt_pad = LANES - SSD_HEADS

    def piece(lo, hi, pad=0):
        wp, bp = w_in[:, :, lo:hi], b_in[:, None, lo:hi]
        if pad:
            wp, bp = jnp.pad(wp, ((0, 0), (0, 0), (0, pad))), jnp.pad(bp, ((0, 0), (0, 0), (0, pad)))
        return wp.astype(BF16), bp

    w_a, b_a = piece(o_qa, o_z)
    w_misc, b_misc = piece(o_z, o_qc, dt_pad)
    w_c, b_c = piece(o_qc, o_gate)
    w_g, b_g = piece(o_gate, o_end)
    p_swa, p_ssd, p_dil, w_o = (m.astype(BF16) for m in (proj_swa, proj_ssd, proj_dil, w_out))
    wr_all = jnp.concatenate(
        [router_group_w, router_expert_w.transpose(0, 2, 1, 3).reshape(depth, d, N_EXPERTS)], axis=2)
    wr_all = jnp.pad(wr_all, ((0, 0), (0, 0), (0, LANES - N_GROUPS - N_EXPERTS))).astype(BF16)
    br_all = jnp.pad(jnp.concatenate([router_group_b, router_expert_b.reshape(depth, N_EXPERTS)], axis=1),
                     ((0, 0), (0, LANES - N_GROUPS - N_EXPERTS)))

    xf = x.reshape(t, d)
    xb = xf.astype(BF16)
    for l in range(depth):
        qkv_a = _proj(xb, w_a, b_a, l, 0, n_a, BF16, tmp, n_a, seq, rope=rope, rope_blocks=(A_Q + A_KV) // LANES,
                      dup_from=A_Q // LANES, name="proj_swa_qkv")
        qk_c = _proj(xb, w_c, b_c, l, 0, 2 * C_QKV, F32, tmp, C_QKV, seq, rope=rope, rope_blocks=C_QKV // LANES, name="proj_dil_qk")
        v_c = _proj(xb, w_c, b_c, l, 2 * C_QKV, C_QKV, F32, tmp, C_QKV, seq, name="proj_dil_v")
        misc = _proj(xb, w_misc, b_misc, l, 0, n_misc + dt_pad, F32, tmp, (n_misc + dt_pad) // 3, seq, name="proj_ssd")
        gates = _proj(xb, w_g, b_g, l, 0, n_gate, F32, tmp, 1024, seq, act="sigmoid", name="proj_gates")

        y_swa = _swa_attention(qkv_a, attn_sinks[l], batch, seq)
        y_ssd = _ssd_mixer(misc, conv_w[l], conv_b[l], dt_bias[l], a_log[l], d_skip[l], ssd_norm_w[l], batch, seq)
        dil = [_dilated_attention(qk_c, v_c, batch, seq, gi) for gi in range(len(DIL_CONFIGS))]

        merged = _merge(y_swa, y_ssd, dil, gates, p_swa[l], p_ssd[l], p_dil[l], tm, 1024)
        x1, ids, comb, cnt = _ln1_router(xf, merged, w_o[l], ln1_g[l], ln1_b[l], wr_all[l], br_all[l].reshape(1, LANES), alpha, tm)

        plan = _dispatch_plan(ids, cnt[0, N_GROUPS:N_GROUPS + N_EXPERTS].astype(I32), t)
        y2 = _moe_experts(x1, plan, expert_w1, expert_w3, expert_w2, l)
        xf, xb = _ln2(x1, y2, comb, ln2_g[l], ln2_b[l], alpha, tm)
    return xf.reshape(batch, seq, d)
```

```python
import functools

import jax
import jax.numpy as jnp
from jax import lax
from jax.experimental import pallas as pl
from jax.experimental.pallas import tpu as pltpu

F32 = jnp.float32
BF16 = jnp.bfloat16
I32 = jnp.int32

LANES = 128
SUBLANES = 8
VMEM_LIMIT_BYTES = 48 * 1024 * 1024

HEAD_DIM = 64
HALF_HEAD = HEAD_DIM // 2
ROPE_THETA = 10000.0
ATTN_BLOCK = 128
LN_EPS = 1e-5

SWA_Q_HEADS = 16
SWA_KV_HEADS = 2
SWA_WINDOW = 128

SSD_HEADS = 24
SSD_HEADDIM = 64
SSD_INNER = SSD_HEADS * SSD_HEADDIM
SSD_GROUPS = 4
SSD_STATE = 128
SSD_CONV = 4
SSD_CHUNK = 128
SSD_HEADS_PER_GROUP = SSD_HEADS // SSD_GROUPS
SSD_GROUP_WIDTH = SSD_INNER // SSD_GROUPS
SSD_BC_WIDTH = 2 * SSD_GROUPS * SSD_STATE

DIL_CONFIGS = ((128, 1), (512, 4), (2048, 16))
DIL_HEADS_PER_GROUP = 6
DIL_HEADS = DIL_HEADS_PER_GROUP * len(DIL_CONFIGS)

A_Q = SWA_Q_HEADS * HEAD_DIM
A_KV = SWA_KV_HEADS * HEAD_DIM
A_KV_DUP = 2 * A_KV
C_QKV = DIL_HEADS * HEAD_DIM
C_OUT = DIL_HEADS_PER_GROUP * HEAD_DIM

N_GROUPS = 4
EXPERTS_PER_GROUP = 8
N_EXPERTS = N_GROUPS * EXPERTS_PER_GROUP
TOP_K = 2
D_EXPERT = 512
MOE_BLOCK = 128
MOE_DMA_UNROLL = 8
ROW_PIECES = 16

NEG_INF = float("-inf")


def _cparams(*sem):
    return pltpu.CompilerParams(dimension_semantics=sem, vmem_limit_bytes=VMEM_LIMIT_BYTES)


def _sigmoid(v):
    return 1.0 / (1.0 + jnp.exp(-v))


def _proj_kernel(*refs, rope_blocks, dup_from, act):
    if rope_blocks:
        x_ref, w_ref, b_ref, cos_ref, sin_ref, o_ref = refs
    else:
        x_ref, w_ref, b_ref, o_ref = refs
    acc = jnp.dot(x_ref[...], w_ref[...], preferred_element_type=F32) + b_ref[...]
    n_blocks = acc.shape[1] // LANES
    if rope_blocks or dup_from < n_blocks:
        lane = lax.broadcasted_iota(I32, (acc.shape[0], LANES), 1)
        first_half = (lane % HEAD_DIM) < HALF_HEAD
        low_head = lane < HEAD_DIM
        out_c = 0
        for c in range(n_blocks):
            v = acc[:, c * LANES:(c + 1) * LANES]
            if c < rope_blocks:
                partner = jnp.where(first_half, pltpu.roll(v, LANES - HALF_HEAD, 1), pltpu.roll(v, HALF_HEAD, 1))
                v = v * cos_ref[...] + partner * sin_ref[...]
            if c >= dup_from:
                other = pltpu.roll(v, HEAD_DIM, 1)
                outs = (jnp.where(low_head, v, other), jnp.where(low_head, other, v))
            else:
                outs = (v,)
            for o in outs:
                o_ref[:, out_c * LANES:(out_c + 1) * LANES] = o.astype(o_ref.dtype)
                out_c += 1
    elif act == "sigmoid":
        o_ref[...] = _sigmoid(acc).astype(o_ref.dtype)
    else:
        o_ref[...] = acc.astype(o_ref.dtype)


def _proj(xb, w, b, layer, col0, n, out_dtype, tm, tn, seq, rope=None, rope_blocks=0, dup_from=None, act=None, name="proj"):
    t, d = xb.shape
    assert t % tm == 0 and n % tn == 0 and seq % tm == 0 and col0 % tn == 0
    j0 = col0 // tn
    in_blocks = tn // LANES
    if dup_from is None:
        dup_from = in_blocks
    assert dup_from == in_blocks or n == tn
    tn_out = tn + (in_blocks - dup_from) * LANES
    in_specs = [
        pl.BlockSpec((tm, d), lambda j, i: (i, 0)),
        pl.BlockSpec((None, d, tn), lambda j, i: (layer, 0, j0 + j)),
        pl.BlockSpec((None, 1, tn), lambda j, i: (layer, 0, j0 + j)),
    ]
    args = [xb, w, b]
    if rope_blocks:
        sblocks = seq // tm
        in_specs += [pl.BlockSpec((tm, LANES), lambda j, i: (i % sblocks, 0))] * 2
        args += list(rope)
    return pl.pallas_call(
        functools.partial(_proj_kernel, rope_blocks=rope_blocks, dup_from=dup_from, act=act),
        grid=(n // tn, t // tm),
        in_specs=in_specs,
        out_specs=pl.BlockSpec((tm, tn_out), lambda j, i: (i, j)),
        out_shape=jax.ShapeDtypeStruct((t, n // tn * tn_out), out_dtype),
        compiler_params=_cparams("arbitrary", "arbitrary"),
        name=name,
    )(*args)


def _band_biases(max_dist):
    row = lax.broadcasted_iota(I32, (ATTN_BLOCK, 2 * ATTN_BLOCK), 0)
    col = lax.broadcasted_iota(I32, (ATTN_BLOCK, 2 * ATTN_BLOCK), 1)
    dist = row + ATTN_BLOCK - col
    in_band = (dist >= 0) & (dist <= max_dist)
    bias_any = jnp.where(in_band, 0.0, NEG_INF).astype(F32)
    bias_first = jnp.where(in_band & (col >= ATTN_BLOCK), 0.0, NEG_INF).astype(F32)
    return bias_any, bias_first


def _attend_pair(q2, k2, v2, bias, sinks):
    lane = lax.broadcasted_iota(I32, (ATTN_BLOCK, LANES), 1)
    low = lane < HEAD_DIM
    scale = HEAD_DIM ** -0.5
    outs, lses = [], []
    for half in range(2):
        qh = jnp.where(low if half == 0 else jnp.logical_not(low), q2, jnp.zeros_like(q2))
        s = lax.dot_general(qh, k2, (((1,), (1,)), ((), ())), preferred_element_type=F32) * scale
        s = s + bias
        m = jnp.max(s, axis=-1, keepdims=True)
        if sinks is not None:
            m = jnp.maximum(m, sinks[half])
        p = jnp.exp(s - m)
        den = jnp.sum(p, axis=-1, keepdims=True)
        if sinks is not None:
            den = den + jnp.exp(sinks[half] - m)
        outs.append(jnp.dot(p.astype(BF16), v2, preferred_element_type=F32) / den)
        lses.append(m + jnp.log(den))
    return jnp.where(low, outs[0], outs[1]), jnp.where(low, lses[0], lses[1])


def _swa_attn_kernel(sink_ref, q_ref, kc_ref, kp_ref, vc_ref, vp_ref, o_ref, kf, vf, *, pairs_per_kv, tq):
    n = pl.program_id(1)
    kf[0:ATTN_BLOCK, :] = kp_ref[...]
    kf[ATTN_BLOCK:, :] = kc_ref[...]
    vf[0:ATTN_BLOCK, :] = vp_ref[...]
    vf[ATTN_BLOCK:, :] = vc_ref[...]
    bias_any, bias_first = _band_biases(SWA_WINDOW - 1)

    def sub_block(j, carry):
        r0 = pl.multiple_of(j * ATTN_BLOCK, ATTN_BLOCK)
        bias = jnp.where(jnp.logical_and(n == 0, j == 0), bias_first, bias_any)
        for hp in range(SWA_Q_HEADS // 2):
            cols = slice(hp * LANES, (hp + 1) * LANES)
            kv_cols = slice((hp // pairs_per_kv) * LANES, (hp // pairs_per_kv + 1) * LANES)
            o, _ = _attend_pair(q_ref[pl.ds(r0, ATTN_BLOCK), cols], kf[pl.ds(r0, 2 * ATTN_BLOCK), kv_cols],
                                vf[pl.ds(r0, 2 * ATTN_BLOCK), kv_cols], bias, (sink_ref[2 * hp], sink_ref[2 * hp + 1]))
            o_ref[pl.ds(r0, ATTN_BLOCK), cols] = o.astype(o_ref.dtype)
        return carry

    lax.fori_loop(0, tq // ATTN_BLOCK, sub_block, 0)


def _dil_attn_kernel(q_ref, kc_ref, kp_ref, vc_ref, vp_ref, o_ref, l_ref, *, dil, max_dist):
    n = pl.program_id(1)
    rows = q_ref.shape[0]
    sub_rows = rows // dil
    bias_any, bias_first = _band_biases(max_dist)
    bias_head = jnp.where(n == 0, bias_first, bias_any)

    def rows_of(ref, r, start, count):
        if dil == 1:
            return ref[start:start + count, :]
        return ref[pl.ds(start * dil + r, count, stride=dil), :]

    for r in range(dil):
        k_prev = rows_of(kp_ref, r, 0, ATTN_BLOCK).astype(BF16)
        v_prev = rows_of(vp_ref, r, 0, ATTN_BLOCK).astype(BF16)
        for j in range(sub_rows // ATTN_BLOCK):
            start = j * ATTN_BLOCK
            k_cur = rows_of(kc_ref, r, start, ATTN_BLOCK).astype(BF16)
            v_cur = rows_of(vc_ref, r, start, ATTN_BLOCK).astype(BF16)
            q2 = rows_of(q_ref, r, start, ATTN_BLOCK).astype(BF16)
            o, l = _attend_pair(q2, jnp.concatenate([k_prev, k_cur], axis=0), jnp.concatenate([v_prev, v_cur], axis=0),
                                bias_head if j == 0 else bias_any, None)
            if dil == 1:
                o_ref[start:start + ATTN_BLOCK, :] = o
                l_ref[start:start + ATTN_BLOCK, :] = l
            else:
                o_ref[pl.ds(start * dil + r, ATTN_BLOCK, stride=dil), :] = o
                l_ref[pl.ds(start * dil + r, ATTN_BLOCK, stride=dil), :] = l
            k_prev, v_prev = k_cur, v_cur


def _attn_tile(length):
    return min(512, length)


def _swa_attention(qkv, sinks, batch, seq):
    tq = _attn_tile(seq)
    qkv = qkv.reshape(batch, seq, A_Q + 2 * A_KV_DUP)
    k_blk = A_Q // A_KV_DUP
    v_blk = k_blk + 1
    sub = tq // ATTN_BLOCK

    def prev(n):
        return jnp.maximum(n * sub - 1, 0)

    kern = functools.partial(_swa_attn_kernel, pairs_per_kv=SWA_Q_HEADS // SWA_KV_HEADS // 2, tq=tq)
    return pl.pallas_call(
        kern,
        grid=(batch, seq // tq),
        in_specs=[
            pl.BlockSpec(memory_space=pltpu.SMEM),
            pl.BlockSpec((None, tq, A_Q), lambda b, n: (b, n, 0)),
            pl.BlockSpec((None, tq, A_KV_DUP), lambda b, n: (b, n, k_blk)),
            pl.BlockSpec((None, ATTN_BLOCK, A_KV_DUP), lambda b, n: (b, prev(n), k_blk)),
            pl.BlockSpec((None, tq, A_KV_DUP), lambda b, n: (b, n, v_blk)),
            pl.BlockSpec((None, ATTN_BLOCK, A_KV_DUP), lambda b, n: (b, prev(n), v_blk)),
        ],
        out_specs=pl.BlockSpec((None, tq, A_Q), lambda b, n: (b, n, 0)),
        out_shape=jax.ShapeDtypeStruct((batch, seq, A_Q), BF16),
        scratch_shapes=[pltpu.VMEM((tq + ATTN_BLOCK, A_KV_DUP), BF16), pltpu.VMEM((tq + ATTN_BLOCK, A_KV_DUP), BF16)],
        compiler_params=_cparams("arbitrary", "arbitrary"),
        name="swa_attn",
    )(sinks, qkv, qkv, qkv, qkv, qkv).reshape(batch * seq, A_Q)


def _dilated_attention(qk, v, batch, seq, gi):
    window, dil = DIL_CONFIGS[gi]
    back = ATTN_BLOCK * dil
    rows = max(_attn_tile(seq), back)
    assert seq % rows == 0 and rows % back == 0
    pairs = DIL_HEADS_PER_GROUP // 2
    q_blk, k_blk, v_blk = gi * pairs, (C_QKV // LANES) + gi * pairs, gi * pairs
    per_seq = seq // rows
    ratio = rows // back

    def cur(col0):
        return pl.BlockSpec((rows, LANES), lambda b, n, hp: (b * per_seq + n, col0 + hp))

    def prev(col0):
        return pl.BlockSpec((back, LANES), lambda b, n, hp: (b * per_seq * ratio + jnp.maximum(n * ratio - 1, 0), col0 + hp))

    out_sds = jax.ShapeDtypeStruct((batch * seq, C_OUT), F32)
    out_spec = pl.BlockSpec((rows, LANES), lambda b, n, hp: (b * per_seq + n, hp))
    return pl.pallas_call(
        functools.partial(_dil_attn_kernel, dil=dil, max_dist=window // dil),
        grid=(batch, per_seq, pairs),
        in_specs=[cur(q_blk), cur(k_blk), prev(k_blk), cur(v_blk), prev(v_blk)],
        out_specs=[out_spec, out_spec],
        out_shape=[out_sds, out_sds],
        compiler_params=_cparams("arbitrary", "arbitrary", "arbitrary"),
        name=f"dil_attn_{dil}",
    )(qk, qk, qk, v, v)


def _ssd_kernel(z_ref, xs_ref, bc_ref, dt_ref, cwx_ref, cwbc_ref, cbx_ref, cbbc_ref, dtb_ref, alog_ref,
                dsk_ref, nw_ref, o_ref, extx, extbc, hst):
    q = SSD_CHUNK
    c = pl.program_id(1)

    @pl.when(c == 0)
    def _():
        extx[0:SUBLANES, :] = jnp.zeros((SUBLANES, SSD_INNER), F32)
        extbc[0:SUBLANES, :] = jnp.zeros((SUBLANES, SSD_BC_WIDTH), F32)
        hst[...] = jnp.zeros(hst.shape, F32)

    extx[SUBLANES:, :] = xs_ref[...]
    extbc[SUBLANES:, :] = bc_ref[...]

    def conv_silu(ext, cw_ref, cb_ref):
        acc = cb_ref[...]
        for k in range(SSD_CONV):
            start = SUBLANES - (SSD_CONV - 1) + k
            acc = acc + cw_ref[k:k + 1, :] * ext[start:start + q, :]
        return acc * _sigmoid(acc)

    xa = conv_silu(extx, cwx_ref, cbx_ref)
    bca = conv_silu(extbc, cwbc_ref, cbbc_ref)
    extx[0:SUBLANES, :] = xs_ref[q - SUBLANES:q, :]
    extbc[0:SUBLANES, :] = bc_ref[q - SUBLANES:q, :]

    dtv = dt_ref[...] + dtb_ref[...]
    dt = jnp.maximum(dtv, 0.0) + jnp.log(1.0 + jnp.exp(-jnp.abs(dtv)))
    a = -jnp.exp(alog_ref[...])
    rr = lax.broadcasted_iota(I32, (q, q), 0)
    cc = lax.broadcasted_iota(I32, (q, q), 1)
    causal = rr >= cc
    tril = jnp.where(causal, 1.0, 0.0).astype(F32)
    acs = jnp.dot(tril, dt * a, preferred_element_type=F32, precision=lax.Precision.HIGHEST)
    acs_t = acs.T

    low = lax.broadcasted_iota(I32, (q, LANES), 1) < SSD_HEADDIM
    low_state = lax.broadcasted_iota(I32, (SSD_STATE, LANES), 1) < SSD_HEADDIM
    pairs_per_group = SSD_HEADS_PER_GROUP // 2

    def per_head(h0, fn):
        return jnp.where(low, fn(h0), fn(h0 + 1))

    for g in range(SSD_GROUPS):
        bg = bca[:, g * SSD_STATE:(g + 1) * SSD_STATE]
        cg = bca[:, (SSD_GROUPS + g) * SSD_STATE:(SSD_GROUPS + g + 1) * SSD_STATE].astype(BF16)
        bt = bg.T.astype(BF16)
        cb = jnp.dot(cg, bt, preferred_element_type=F32)
        ys = []
        for pi in range(pairs_per_group):
            pair = g * pairs_per_group + pi
            h0 = 2 * pair
            cols = slice(pair * LANES, (pair + 1) * LANES)
            xp = xa[:, cols]
            xdt = xp * per_head(h0, lambda h: dt[:, h:h + 1])
            m_both = jnp.concatenate(
                [(cb * jnp.exp(jnp.where(causal, acs[:, h:h + 1] - acs_t[h:h + 1, :], NEG_INF))).astype(BF16)
                 for h in (h0, h0 + 1)], axis=1)
            x_both = jnp.concatenate([jnp.where(low, xdt, 0.0), jnp.where(low, 0.0, xdt)], axis=0).astype(BF16)
            y = jnp.dot(m_both, x_both, preferred_element_type=F32)
            hprev = hst[pair]
            y = y + jnp.dot(cg, hprev.astype(BF16), preferred_element_type=F32) * per_head(h0, lambda h: jnp.exp(acs[:, h:h + 1]))
            y = y + xp * dsk_ref[:, cols]
            to_end = per_head(h0, lambda h: jnp.exp(acs[q - 1:q, h:h + 1] - acs[:, h:h + 1]))
            chunk_decay = jnp.where(low_state, jnp.exp(acs[q - 1:q, h0:h0 + 1]), jnp.exp(acs[q - 1:q, h0 + 1:h0 + 2]))
            hst[pair] = hprev * chunk_decay + jnp.dot(bt, (xdt * to_end).astype(BF16), preferred_element_type=F32)
            ys.append(y)
        lo, hi = g * SSD_GROUP_WIDTH, (g + 1) * SSD_GROUP_WIDTH
        zg = z_ref[:, lo:hi]
        yg = jnp.concatenate(ys, axis=1) * (zg * _sigmoid(zg))
        yg = yg * lax.rsqrt(jnp.mean(yg * yg, axis=-1, keepdims=True) + LN_EPS)
        o_ref[:, lo:hi] = (yg * nw_ref[:, lo:hi]).astype(o_ref.dtype)


def _ssd_mixer(misc, conv_w, conv_b, dt_bias, a_log, d_skip, norm_w, batch, seq):
    q = SSD_CHUNK
    nc = seq // q
    pad = LANES - SSD_HEADS
    row = lambda v: v.reshape(1, -1)
    args = [
        misc, misc, misc, misc,
        conv_w[:, :SSD_INNER], conv_w[:, SSD_INNER:], row(conv_b[:SSD_INNER]), row(conv_b[SSD_INNER:]),
        row(jnp.pad(dt_bias, (0, pad))), row(jnp.pad(a_log, (0, pad))),
        row(jnp.repeat(d_skip, SSD_HEADDIM)), row(norm_w),
    ]
    tok = lambda b, c: b * nc + c
    full = lambda shape: pl.BlockSpec(shape, lambda b, c: (0, 0))
    bc_blk = 2 * SSD_INNER // SSD_BC_WIDTH
    dt_blk = (2 * SSD_INNER + SSD_BC_WIDTH) // LANES
    return pl.pallas_call(
        _ssd_kernel,
        grid=(batch, nc),
        in_specs=[
            pl.BlockSpec((q, SSD_INNER), lambda b, c: (tok(b, c), 0)),
            pl.BlockSpec((q, SSD_INNER), lambda b, c: (tok(b, c), 1)),
            pl.BlockSpec((q, SSD_BC_WIDTH), lambda b, c: (tok(b, c), bc_blk)),
            pl.BlockSpec((q, LANES), lambda b, c: (tok(b, c), dt_blk)),
            full((SSD_CONV, SSD_INNER)), full((SSD_CONV, SSD_BC_WIDTH)), full((1, SSD_INNER)), full((1, SSD_BC_WIDTH)),
            full((1, LANES)), full((1, LANES)), full((1, SSD_INNER)), full((1, SSD_INNER)),
        ],
        out_specs=pl.BlockSpec((q, SSD_INNER), lambda b, c: (tok(b, c), 0)),
        out_shape=jax.ShapeDtypeStruct((batch * seq, SSD_INNER), BF16),
        scratch_shapes=[
            pltpu.VMEM((SUBLANES + q, SSD_INNER), F32),
            pltpu.VMEM((SUBLANES + q, SSD_BC_WIDTH), F32),
            pltpu.VMEM((SSD_HEADS // 2, SSD_STATE, LANES), F32),
        ],
        compiler_params=_cparams("arbitrary", "arbitrary"),
        name="ssd_mixer",
    )(*args)


def _merge_kernel(ysw_ref, yss_ref, o0_ref, l0_ref, o1_ref, l1_ref, o2_ref, l2_ref, g0_ref, g1_ref, g2_ref,
                  p0_ref, p1_ref, p2_ref, out_ref):
    l0, l1, l2 = l0_ref[...], l1_ref[...], l2_ref[...]
    m = jnp.maximum(jnp.maximum(l0, l1), l2)
    e0, e1, e2 = jnp.exp(l0 - m), jnp.exp(l1 - m), jnp.exp(l2 - m)
    inv = 1.0 / (e0 + e1 + e2)
    ydil = (o0_ref[...] * (e0 * inv) + o1_ref[...] * (e1 * inv) + o2_ref[...] * (e2 * inv)).astype(BF16)
    merged = g0_ref[...] * jnp.dot(ysw_ref[...], p0_ref[...], preferred_element_type=F32)
    merged = merged + g1_ref[...] * jnp.dot(yss_ref[...], p1_ref[...], preferred_element_type=F32)
    merged = merged + g2_ref[...] * jnp.dot(ydil, p2_ref[...], preferred_element_type=F32)
    out_ref[...] = merged.astype(out_ref.dtype)


def _merge(y_swa, y_ssd, dil, gates, p_swa, p_ssd, p_dil, tm, tn):
    t = y_swa.shape[0]
    d = p_swa.shape[1]
    nj = d // tn
    rows = lambda width: pl.BlockSpec((tm, width), lambda i, j: (i, 0))
    gate = lambda k: pl.BlockSpec((tm, tn), lambda i, j: (i, k * nj + j))
    wcol = lambda width: pl.BlockSpec((width, tn), lambda i, j: (0, j))
    dil_args = [a for pair in dil for a in pair]
    return pl.pallas_call(
        _merge_kernel,
        grid=(t // tm, nj),
        in_specs=[rows(A_Q), rows(SSD_INNER)] + [rows(C_OUT)] * 6 + [gate(0), gate(1), gate(2)]
        + [wcol(A_Q), wcol(SSD_INNER), wcol(C_OUT)],
        out_specs=pl.BlockSpec((tm, tn), lambda i, j: (i, j)),
        out_shape=jax.ShapeDtypeStruct((t, d), BF16),
        compiler_params=_cparams("arbitrary", "arbitrary"),
        name="merge",
    )(y_swa, y_ssd, *dil_args, gates, gates, gates, p_swa, p_ssd, p_dil)


def _rows_from_contiguous(ref, n_rows):
    return jnp.concatenate([ref[pl.ds(j, n_rows, stride=ROW_PIECES), :] for j in range(ROW_PIECES)], axis=1)


def _rows_to_contiguous(ref, first_row, val):
    for j in range(ROW_PIECES):
        ref[pl.ds(first_row * ROW_PIECES + j, val.shape[0], stride=ROW_PIECES), :] = val[:, j * LANES:(j + 1) * LANES]


def _layer_norm(h, g, b):
    mu = jnp.mean(h, axis=-1, keepdims=True)
    hc = h - mu
    var = jnp.mean(hc * hc, axis=-1, keepdims=True)
    return hc * lax.rsqrt(var + LN_EPS) * g + b


LN1_ROW_SPLIT = 2


def _ln1_router_kernel(x_ref, m_ref, w_ref, g_ref, b_ref, wr_ref, br_ref, xo_ref, xc_ref, ids_ref, comb_ref, cnt_ref, *, alpha):
    i = pl.program_id(0)
    tm = x_ref.shape[0]
    rows = tm // LN1_ROW_SPLIT
    lane = lax.broadcasted_iota(I32, (rows, LANES), 1).astype(F32)
    big = float(2 * LANES)

    def first_argmax(vals):
        top = jnp.max(vals, axis=-1, keepdims=True)
        return top, jnp.min(jnp.where(vals == top, lane, big), axis=-1, keepdims=True)

    tops = []
    for part in range(LN1_ROW_SPLIT):
        sl = slice(part * rows, (part + 1) * rows)
        h = alpha * x_ref[sl, :] + jnp.dot(m_ref[sl, :], w_ref[...], preferred_element_type=F32)
        y = _layer_norm(h, g_ref[...], b_ref[...])
        xo_ref[sl, :] = y
        _rows_to_contiguous(xc_ref, part * rows, y)

        logits = jnp.dot(y.astype(BF16), wr_ref[...], preferred_element_type=F32) + br_ref[...]
        gl = jnp.where(lane < N_GROUPS, logits, NEG_INF)
        gmax, gsel = first_argmax(gl)
        p_group = 1.0 / jnp.sum(jnp.exp(gl - gmax), axis=-1, keepdims=True)
        lo = N_GROUPS + gsel * EXPERTS_PER_GROUP
        el = jnp.where((lane >= lo) & (lane < lo + EXPERTS_PER_GROUP), logits, NEG_INF)
        v1, i1 = first_argmax(el)
        v2, i2 = first_argmax(jnp.where(lane == i1, NEG_INF, el))
        e21 = jnp.exp(v2 - v1)
        inv = 1.0 / (1.0 + e21)
        comb_ref[sl, :] = jnp.where(lane == 0.0, inv * p_group, jnp.where(lane == 1.0, e21 * inv * p_group, 0.0))
        tops.append((i1, i2))

    @pl.when(i == 0)
    def _():
        cnt_ref[...] = jnp.zeros(cnt_ref.shape, F32)

    i1 = jnp.concatenate([t[0] for t in tops], axis=0)
    i2 = jnp.concatenate([t[1] for t in tops], axis=0)
    lane = lax.broadcasted_iota(I32, (tm, LANES), 1).astype(F32)
    pick1 = jnp.where(lane == i1, 1.0, 0.0)
    pick2 = jnp.where(lane == i2, 1.0, 0.0)
    picks = pick1 + pick2
    earlier = lax.broadcasted_iota(I32, (tm, tm), 0) > lax.broadcasted_iota(I32, (tm, tm), 1)
    before = jnp.dot(jnp.where(earlier, 1.0, 0.0).astype(BF16), picks.astype(BF16), preferred_element_type=F32) + cnt_ref[...]
    rank1 = jnp.sum(pick1 * before, axis=-1, keepdims=True)
    rank2 = jnp.sum(pick2 * before, axis=-1, keepdims=True)
    cnt_ref[...] = cnt_ref[...] + jnp.sum(picks, axis=0, keepdims=True)
    ids = jnp.where(lane == 0.0, i1 - N_GROUPS, jnp.where(lane == 1.0, i2 - N_GROUPS,
                    jnp.where(lane == 2.0, rank1, jnp.where(lane == 3.0, rank2, 0.0))))
    for c in range(tm // LANES):
        ids_ref[:, c * LANES:(c + 1) * LANES] = ids[c * LANES:(c + 1) * LANES, :].T[0:SUBLANES, :].astype(I32)


def _ln1_router(x, merged, w_out, g, b, wr, br, alpha, tm):
    t, d = x.shape
    rows = lambda width: pl.BlockSpec((tm, width), lambda i: (i, 0))
    full = lambda shape: pl.BlockSpec(shape, lambda i: (0, 0))
    once = lambda shape: pl.BlockSpec(shape, lambda i: (0, 0), pipeline_mode=pl.Buffered(1))
    return pl.pallas_call(
        functools.partial(_ln1_router_kernel, alpha=alpha),
        grid=(t // tm,),
        in_specs=[rows(d), rows(d), once((d, d)), full((1, d)), full((1, d)), once((d, LANES)), full((1, LANES))],
        out_specs=[rows(d), pl.BlockSpec((tm * ROW_PIECES, LANES), lambda i: (i, 0)),
                   pl.BlockSpec((SUBLANES, tm), lambda i: (0, i)), rows(LANES), full((1, LANES))],
        out_shape=[jax.ShapeDtypeStruct((t, d), F32), jax.ShapeDtypeStruct((t * ROW_PIECES, LANES), F32),
                   jax.ShapeDtypeStruct((SUBLANES, t), I32), jax.ShapeDtypeStruct((t, LANES), F32),
                   jax.ShapeDtypeStruct((1, LANES), F32)],
        compiler_params=_cparams("arbitrary"),
        name="ln1_router",
    )(x, merged, w_out, g.reshape(1, d), b.reshape(1, d), wr, br)


def _moe_kernel(be_ref, dest_ref, pad_lo_ref, pad_hi_ref, nu_ref, x_hbm, w1_ref, w3_ref, w2_ref, y_hbm,
                xbuf0, xbuf1, ybuf0, ybuf1, w1b, w3b, w2b, tok_ref, dst_ref, gsem, ssem, *, n_tok):
    b = pl.program_id(0)
    n_used = nu_ref[0]
    xbufs, ybufs = (xbuf0, xbuf1), (ybuf0, ybuf1)
    trash_row0 = TOP_K * n_tok

    def entry(blk, r):
        return (blk + 1) * MOE_BLOCK + r

    def gather_copy(blk, p, r):
        src = pl.multiple_of(tok_ref[entry(blk, r)], ROW_PIECES)
        return pltpu.make_async_copy(
            x_hbm.at[pl.ds(src, ROW_PIECES), :], xbufs[p].at[pl.ds(r * ROW_PIECES, ROW_PIECES), :], gsem.at[p])

    def scatter_copy(blk, p, r):
        dst = pl.multiple_of(dst_ref[entry(blk, r)], ROW_PIECES)
        return pltpu.make_async_copy(
            ybufs[p].at[pl.ds(r * ROW_PIECES, ROW_PIECES), :], y_hbm.at[pl.ds(dst, ROW_PIECES), :], ssem.at[p])

    def for_rows(fn):
        def chunk(c, carry):
            for u in range(MOE_DMA_UNROLL):
                fn(c * MOE_DMA_UNROLL + u)
            return carry
        lax.fori_loop(0, MOE_BLOCK // MOE_DMA_UNROLL, chunk, 0)

    @pl.when(b == 0)
    def _():
        def pad_entries(lo, hi):
            def pad_entry(s, carry):
                tok_ref[s] = 0
                dst_ref[s] = (trash_row0 + lax.bitwise_and(s, MOE_BLOCK - 1)) * ROW_PIECES
                return carry
            lax.fori_loop(lo, hi, pad_entry, 0)

        pad_entries(0, MOE_BLOCK)
        lax.fori_loop(0, N_EXPERTS, lambda e, c: (pad_entries(entry(0, pad_lo_ref[e]), entry(0, pad_hi_ref[e])), c)[1], 0)
        pad_entries(entry(n_used, 0), entry(n_used + 1, 0))

        for k in range(TOP_K):
            def fill(c, carry):
                for u in range(MOE_DMA_UNROLL):
                    tkn = c * MOE_DMA_UNROLL + u
                    s = dest_ref[k * n_tok + tkn] + MOE_BLOCK
                    tok_ref[s] = tkn * ROW_PIECES
                    dst_ref[s] = (k * n_tok + tkn) * ROW_PIECES
                return carry
            lax.fori_loop(0, n_tok // MOE_DMA_UNROLL, fill, 0)

        ybuf1[...] = jnp.zeros(ybuf1.shape, F32)
        for_rows(lambda r: gather_copy(0, 0, r).start())

    def step(p):
        q = 1 - p
        for_rows(lambda r: gather_copy(b, p, r).wait())

        @pl.when(b >= 1)
        def _():
            for_rows(lambda r: scatter_copy(b - 2, p, r).wait())

        @pl.when(jnp.logical_or(b == 0, be_ref[b] != be_ref[jnp.maximum(b - 1, 0)]))
        def _():
            w1b[...] = w1_ref[...].astype(BF16)
            w3b[...] = w3_ref[...].astype(BF16)
            w2b[...] = w2_ref[...].astype(BF16)

        def issue(part):
            def row(r, carry):
                gather_copy(b + 1, q, r).start()
                scatter_copy(b - 1, q, r).start()
                return carry
            lax.fori_loop(part * MOE_BLOCK // 4, (part + 1) * MOE_BLOCK // 4, row, 0, unroll=True)

        xb = _rows_from_contiguous(xbufs[p], MOE_BLOCK).astype(BF16)
        issue(0)
        h1 = jnp.dot(xb, w1b[...], preferred_element_type=F32)
        issue(1)
        h3 = jnp.dot(xb, w3b[...], preferred_element_type=F32)
        issue(2)
        h = (h1 * _sigmoid(h1) * h3).astype(BF16)
        issue(3)
        _rows_to_contiguous(ybufs[p], 0, jnp.dot(h, w2b[...], preferred_element_type=F32))

        @pl.when(b == n_used - 1)
        def _():
            for_rows(lambda r: scatter_copy(b - 1, q, r).wait())
            for_rows(lambda r: scatter_copy(b, p, r).start())
            for_rows(lambda r: scatter_copy(b, p, r).wait())
            for_rows(lambda r: gather_copy(b + 1, q, r).wait())

    for p in range(2):
        pl.when(jnp.logical_and(b < n_used, b % 2 == p))(functools.partial(step, p))


def _moe_experts(x1, plan, w1, w3, w2, layer):
    t, d = x1.shape[0] // ROW_PIECES, x1.shape[1] * ROW_PIECES
    block_e, dest, pad_lo, pad_hi, n_used = plan
    n_blocks = block_e.shape[0]
    table = pltpu.SMEM(((n_blocks + 2) * MOE_BLOCK,), I32)
    row_buf = pltpu.VMEM((MOE_BLOCK * ROW_PIECES, LANES), F32)
    wspec = lambda shape: pl.BlockSpec((None, None) + shape, lambda b, be, *_: (layer, be[b], 0, 0))
    grid_spec = pltpu.PrefetchScalarGridSpec(
        num_scalar_prefetch=5,
        grid=(n_blocks,),
        in_specs=[pl.BlockSpec(memory_space=pl.ANY), wspec((d, D_EXPERT)), wspec((d, D_EXPERT)), wspec((D_EXPERT, d))],
        out_specs=pl.BlockSpec(memory_space=pl.ANY),
        scratch_shapes=[
            row_buf, row_buf, row_buf, row_buf,
            pltpu.VMEM((d, D_EXPERT), BF16), pltpu.VMEM((d, D_EXPERT), BF16), pltpu.VMEM((D_EXPERT, d), BF16),
            table, table,
            pltpu.SemaphoreType.DMA((2,)),
            pltpu.SemaphoreType.DMA((2,)),
        ],
    )
    return pl.pallas_call(
        functools.partial(_moe_kernel, n_tok=t),
        grid_spec=grid_spec,
        out_shape=jax.ShapeDtypeStruct(((TOP_K * t + MOE_BLOCK) * ROW_PIECES, LANES), F32),
        compiler_params=_cparams("arbitrary"),
        name="moe_experts",
    )(block_e, dest, pad_lo, pad_hi, n_used, x1, w1, w3, w2)


def _ln2_kernel(x_ref, y0_ref, y1_ref, comb_ref, g_ref, b_ref, xo_ref, xb_ref, *, alpha):
    comb = comb_ref[...]
    tm = x_ref.shape[0]
    h = (alpha * x_ref[...] + comb[:, 0:1] * _rows_from_contiguous(y0_ref, tm)
         + comb[:, 1:2] * _rows_from_contiguous(y1_ref, tm))
    y = _layer_norm(h, g_ref[...], b_ref[...])
    xo_ref[...] = y
    xb_ref[...] = y.astype(BF16)


def _ln2(x1, y2, comb, g, b, alpha, tm):
    t, d = x1.shape
    nblk = t // tm
    rows = lambda width: pl.BlockSpec((tm, width), lambda i: (i, 0))
    full = lambda shape: pl.BlockSpec(shape, lambda i: (0, 0))
    return pl.pallas_call(
        functools.partial(_ln2_kernel, alpha=alpha),
        grid=(nblk,),
        in_specs=[rows(d), pl.BlockSpec((tm * ROW_PIECES, LANES), lambda i: (i, 0)),
                  pl.BlockSpec((tm * ROW_PIECES, LANES), lambda i: (nblk + i, 0)), rows(LANES), full((1, d)), full((1, d))],
        out_specs=[rows(d), rows(d)],
        out_shape=[jax.ShapeDtypeStruct((t, d), F32), jax.ShapeDtypeStruct((t, d), BF16)],
        compiler_params=_cparams("arbitrary"),
        name="ln2",
    )(x1, y2, y2, comb, g.reshape(1, d), b.reshape(1, d))


def _dispatch_plan(ids, counts, n_tok):
    n_assign = n_tok * TOP_K
    n_blocks = -(-n_assign // MOE_BLOCK) + N_EXPERTS
    experts = jnp.arange(N_EXPERTS, dtype=I32)
    padded = (counts + MOE_BLOCK - 1) // MOE_BLOCK * MOE_BLOCK
    pad_end = jnp.cumsum(padded)
    pad_start = pad_end - padded
    slot0 = jnp.sum(jnp.where(ids[0:TOP_K, :, None] == experts, pad_start, 0), axis=-1)
    dest = (slot0 + ids[TOP_K:2 * TOP_K]).reshape(n_assign)
    starts = jnp.arange(n_blocks, dtype=I32) * MOE_BLOCK
    block_e = jnp.minimum(jnp.sum((pad_end[None, :] <= starts[:, None]).astype(I32), axis=1), N_EXPERTS - 1)
    return block_e, dest, pad_start + counts, pad_end, pad_end[-1:] // MOE_BLOCK


def _rope_tables(seq):
    inv_freq = 1.0 / (ROPE_THETA ** (jnp.arange(0, HEAD_DIM, 2, dtype=F32) / HEAD_DIM))
    ang = jnp.arange(seq, dtype=F32)[:, None] * inv_freq[None, :]
    cos, sin = jnp.cos(ang), jnp.sin(ang)
    reps = LANES // HEAD_DIM
    return jnp.tile(jnp.concatenate([cos, cos], axis=1), (1, reps)), jnp.tile(jnp.concatenate([-sin, sin], axis=1), (1, reps))


def _tile(total, want):
    return min(total, want)


def kernel(x, w_in, b_in, attn_sinks, conv_w, conv_b, dt_bias, a_log, d_skip, ssd_norm_w, proj_swa, proj_ssd, proj_dil, w_out, ln1_g, ln1_b, router_group_w, router_group_b, router_expert_w, router_expert_b, expert_w1, expert_w3, expert_w2, ln2_g, ln2_b):
    batch, seq, d = x.shape
    depth = w_in.shape[0]
    t = batch * seq
    alpha = (2 * depth) ** 0.25
    rope = _rope_tables(seq)
    tm = _tile(seq, 512)
    tmp = _tile(seq, 1024)

    sizes = (A_Q, A_KV, A_KV, SSD_INNER, SSD_INNER + SSD_BC_WIDTH, SSD_HEADS, C_QKV, C_QKV, C_QKV, 3 * d)
    offs = [0]
    for s in sizes:
        offs.append(offs[-1] + s)
    o_qa, o_ka, o_va, o_z, o_xbc, o_dt, o_qc, o_kc, o_vc, o_gate, o_end = offs

    n_a, n_misc, n_gate = o_z - o_qa, o_qc - o_z, o_end - o_gate
    dt_pad = LANES - SSD_HEADS

    def piece(lo, hi, pad=0):
        wp, bp = w_in[:, :, lo:hi], b_in[:, None, lo:hi]
        if pad:
            wp, bp = jnp.pad(wp, ((0, 0), (0, 0), (0, pad))), jnp.pad(bp, ((0, 0), (0, 0), (0, pad)))
        return wp.astype(BF16), bp

    w_a, b_a = piece(o_qa, o_z)
    w_misc, b_misc = piece(o_z, o_qc, dt_pad)
    w_c, b_c = piece(o_qc, o_gate)
    w_g, b_g = piece(o_gate, o_end)
    p_swa, p_ssd, p_dil, w_o = (m.astype(BF16) for m in (proj_swa, proj_ssd, proj_dil, w_out))
    wr_all = jnp.concatenate(
        [router_group_w, router_expert_w.transpose(0, 2, 1, 3).reshape(depth, d, N_EXPERTS)], axis=2)
    wr_all = jnp.pad(wr_all, ((0, 0), (0, 0), (0, LANES - N_GROUPS - N_EXPERTS))).astype(BF16)
    br_all = jnp.pad(jnp.concatenate([router_group_b, router_expert_b.reshape(depth, N_EXPERTS)], axis=1),
                     ((0, 0), (0, LANES - N_GROUPS - N_EXPERTS)))

    xf = x.reshape(t, d)
    xb = xf.astype(BF16)
    for l in range(depth):
        qkv_a = _proj(xb, w_a, b_a, l, 0, n_a, BF16, tmp, n_a, seq, rope=rope, rope_blocks=(A_Q + A_KV) // LANES,
                      dup_from=A_Q // LANES, name="proj_swa_qkv")
        qk_c = _proj(xb, w_c, b_c, l, 0, 2 * C_QKV, F32, tmp, C_QKV, seq, rope=rope, rope_blocks=C_QKV // LANES, name="proj_dil_qk")
        v_c = _proj(xb, w_c, b_c, l, 2 * C_QKV, C_QKV, F32, tmp, C_QKV, seq, name="proj_dil_v")
        misc = _proj(xb, w_misc, b_misc, l, 0, n_misc + dt_pad, F32, tmp, (n_misc + dt_pad) // 3, seq, name="proj_ssd")
        gates = _proj(xb, w_g, b_g, l, 0, n_gate, F32, tmp, 1024, seq, act="sigmoid", name="proj_gates")

        y_swa = _swa_attention(qkv_a, attn_sinks[l], batch, seq)
        y_ssd = _ssd_mixer(misc, conv_w[l], conv_b[l], dt_bias[l], a_log[l], d_skip[l], ssd_norm_w[l], batch, seq)
        dil = [_dilated_attention(qk_c, v_c, batch, seq, gi) for gi in range(len(DIL_CONFIGS))]

        merged = _merge(y_swa, y_ssd, dil, gates, p_swa[l], p_ssd[l], p_dil[l], tm, 1024)
        x1, x1c, ids, comb, cnt = _ln1_router(xf, merged, w_o[l], ln1_g[l], ln1_b[l], wr_all[l], br_all[l].reshape(1, LANES), alpha, tm)

        plan = _dispatch_plan(ids, cnt[0, N_GROUPS:N_GROUPS + N_EXPERTS].astype(I32), t)
        y2 = _moe_experts(x1c, plan, expert_w1, expert_w3, expert_w2, l)
        xf, xb = _ln2(x1, y2, comb, ln2_g[l], ln2_b[l], alpha, tm)
    return xf.reshape(batch, seq, d)
```

```python
import functools

import jax
import jax.numpy as jnp
from jax import lax
from jax.experimental import pallas as pl
from jax.experimental.pallas import tpu as pltpu

F32 = jnp.float32
BF16 = jnp.bfloat16
I32 = jnp.int32

LANES = 128
SUBLANES = 8
VMEM_LIMIT_BYTES = 48 * 1024 * 1024

HEAD_DIM = 64
HALF_HEAD = HEAD_DIM // 2
ROPE_THETA = 10000.0
ATTN_BLOCK = 128
LN_EPS = 1e-5

SWA_Q_HEADS = 16
SWA_KV_HEADS = 2
SWA_WINDOW = 128

SSD_HEADS = 24
SSD_HEADDIM = 64
SSD_INNER = SSD_HEADS * SSD_HEADDIM
SSD_GROUPS = 4
SSD_STATE = 128
SSD_CONV = 4
SSD_CHUNK = 128
SSD_HEADS_PER_GROUP = SSD_HEADS // SSD_GROUPS
SSD_GROUP_WIDTH = SSD_INNER // SSD_GROUPS
SSD_BC_WIDTH = 2 * SSD_GROUPS * SSD_STATE

DIL_CONFIGS = ((128, 1), (512, 4), (2048, 16))
DIL_HEADS_PER_GROUP = 6
DIL_HEADS = DIL_HEADS_PER_GROUP * len(DIL_CONFIGS)

A_Q = SWA_Q_HEADS * HEAD_DIM
A_KV = SWA_KV_HEADS * HEAD_DIM
A_KV_DUP = 2 * A_KV
C_QKV = DIL_HEADS * HEAD_DIM
C_OUT = DIL_HEADS_PER_GROUP * HEAD_DIM

N_GROUPS = 4
EXPERTS_PER_GROUP = 8
N_EXPERTS = N_GROUPS * EXPERTS_PER_GROUP
TOP_K = 2
D_EXPERT = 512
MOE_BLOCK = 128
MOE_DMA_UNROLL = 8

NEG_INF = float("-inf")


def _cparams(*sem):
    return pltpu.CompilerParams(dimension_semantics=sem, vmem_limit_bytes=VMEM_LIMIT_BYTES)


def _sigmoid(v):
    return 1.0 / (1.0 + jnp.exp(-v))


def _proj_kernel(*refs, rope_blocks, dup_from, act):
    if rope_blocks:
        x_ref, w_ref, b_ref, cos_ref, sin_ref, o_ref = refs
    else:
        x_ref, w_ref, b_ref, o_ref = refs
    acc = jnp.dot(x_ref[...], w_ref[...], preferred_element_type=F32) + b_ref[...]
    n_blocks = acc.shape[1] // LANES
    if rope_blocks or dup_from < n_blocks:
        lane = lax.broadcasted_iota(I32, (acc.shape[0], LANES), 1)
        first_half = (lane % HEAD_DIM) < HALF_HEAD
        low_head = lane < HEAD_DIM
        out_c = 0
        for c in range(n_blocks):
            v = acc[:, c * LANES:(c + 1) * LANES]
            if c < rope_blocks:
                partner = jnp.where(first_half, pltpu.roll(v, LANES - HALF_HEAD, 1), pltpu.roll(v, HALF_HEAD, 1))
                v = v * cos_ref[...] + partner * sin_ref[...]
            if c >= dup_from:
                other = pltpu.roll(v, HEAD_DIM, 1)
                outs = (jnp.where(low_head, v, other), jnp.where(low_head, other, v))
            else:
                outs = (v,)
            for o in outs:
                o_ref[:, out_c * LANES:(out_c + 1) * LANES] = o.astype(o_ref.dtype)
                out_c += 1
    elif act == "sigmoid":
        o_ref[...] = _sigmoid(acc).astype(o_ref.dtype)
    else:
        o_ref[...] = acc.astype(o_ref.dtype)


def _proj(xb, w, b, layer, col0, n, out_dtype, tm, tn, seq, rope=None, rope_blocks=0, dup_from=None, act=None, name="proj"):
    t, d = xb.shape
    assert t % tm == 0 and n % tn == 0 and seq % tm == 0 and col0 % tn == 0
    j0 = col0 // tn
    in_blocks = tn // LANES
    if dup_from is None:
        dup_from = in_blocks
    assert dup_from == in_blocks or n == tn
    tn_out = tn + (in_blocks - dup_from) * LANES
    in_specs = [
        pl.BlockSpec((tm, d), lambda j, i: (i, 0)),
        pl.BlockSpec((None, d, tn), lambda j, i: (layer, 0, j0 + j)),
        pl.BlockSpec((None, 1, tn), lambda j, i: (layer, 0, j0 + j)),
    ]
    args = [xb, w, b]
    if rope_blocks:
        sblocks = seq // tm
        in_specs += [pl.BlockSpec((tm, LANES), lambda j, i: (i % sblocks, 0))] * 2
        args += list(rope)
    return pl.pallas_call(
        functools.partial(_proj_kernel, rope_blocks=rope_blocks, dup_from=dup_from, act=act),
        grid=(n // tn, t // tm),
        in_specs=in_specs,
        out_specs=pl.BlockSpec((tm, tn_out), lambda j, i: (i, j)),
        out_shape=jax.ShapeDtypeStruct((t, n // tn * tn_out), out_dtype),
        compiler_params=_cparams("arbitrary", "arbitrary"),
        name=name,
    )(*args)


def _band_biases(max_dist):
    row = lax.broadcasted_iota(I32, (ATTN_BLOCK, 2 * ATTN_BLOCK), 0)
    col = lax.broadcasted_iota(I32, (ATTN_BLOCK, 2 * ATTN_BLOCK), 1)
    dist = row + ATTN_BLOCK - col
    in_band = (dist >= 0) & (dist <= max_dist)
    bias_any = jnp.where(in_band, 0.0, NEG_INF).astype(F32)
    bias_first = jnp.where(in_band & (col >= ATTN_BLOCK), 0.0, NEG_INF).astype(F32)
    return bias_any, bias_first


def _attend_pair(q2, k2, v2, bias, sinks):
    lane = lax.broadcasted_iota(I32, (ATTN_BLOCK, LANES), 1)
    low = lane < HEAD_DIM
    scale = HEAD_DIM ** -0.5
    outs, lses = [], []
    for half in range(2):
        qh = jnp.where(low if half == 0 else jnp.logical_not(low), q2, jnp.zeros_like(q2))
        s = lax.dot_general(qh, k2, (((1,), (1,)), ((), ())), preferred_element_type=F32) * scale
        s = s + bias
        m = jnp.max(s, axis=-1, keepdims=True)
        if sinks is not None:
            m = jnp.maximum(m, sinks[half])
        p = jnp.exp(s - m)
        den = jnp.sum(p, axis=-1, keepdims=True)
        if sinks is not None:
            den = den + jnp.exp(sinks[half] - m)
        outs.append(jnp.dot(p.astype(BF16), v2, preferred_element_type=F32) / den)
        lses.append(m + jnp.log(den))
    return jnp.where(low, outs[0], outs[1]), jnp.where(low, lses[0], lses[1])


def _swa_attn_kernel(sink_ref, q_ref, kc_ref, kp_ref, vc_ref, vp_ref, o_ref, kf, vf, *, pairs_per_kv, tq):
    n = pl.program_id(1)
    kf[0:ATTN_BLOCK, :] = kp_ref[...]
    kf[ATTN_BLOCK:, :] = kc_ref[...]
    vf[0:ATTN_BLOCK, :] = vp_ref[...]
    vf[ATTN_BLOCK:, :] = vc_ref[...]
    bias_any, bias_first = _band_biases(SWA_WINDOW - 1)

    def sub_block(j, carry):
        r0 = pl.multiple_of(j * ATTN_BLOCK, ATTN_BLOCK)
        bias = jnp.where(jnp.logical_and(n == 0, j == 0), bias_first, bias_any)
        for hp in range(SWA_Q_HEADS // 2):
            cols = slice(hp * LANES, (hp + 1) * LANES)
            kv_cols = slice((hp // pairs_per_kv) * LANES, (hp // pairs_per_kv + 1) * LANES)
            o, _ = _attend_pair(q_ref[pl.ds(r0, ATTN_BLOCK), cols], kf[pl.ds(r0, 2 * ATTN_BLOCK), kv_cols],
                                vf[pl.ds(r0, 2 * ATTN_BLOCK), kv_cols], bias, (sink_ref[2 * hp], sink_ref[2 * hp + 1]))
            o_ref[pl.ds(r0, ATTN_BLOCK), cols] = o.astype(o_ref.dtype)
        return carry

    lax.fori_loop(0, tq // ATTN_BLOCK, sub_block, 0)


def _dil_attn_kernel(q_ref, kc_ref, kp_ref, vc_ref, vp_ref, o_ref, l_ref, *, dil, max_dist):
    n = pl.program_id(1)
    rows = q_ref.shape[0]
    sub_rows = rows // dil
    bias_any, bias_first = _band_biases(max_dist)
    bias_head = jnp.where(n == 0, bias_first, bias_any)

    def rows_of(ref, r, start, count):
        if dil == 1:
            return ref[start:start + count, :]
        return ref[pl.ds(start * dil + r, count, stride=dil), :]

    for r in range(dil):
        k_prev = rows_of(kp_ref, r, 0, ATTN_BLOCK).astype(BF16)
        v_prev = rows_of(vp_ref, r, 0, ATTN_BLOCK).astype(BF16)
        for j in range(sub_rows // ATTN_BLOCK):
            start = j * ATTN_BLOCK
            k_cur = rows_of(kc_ref, r, start, ATTN_BLOCK).astype(BF16)
            v_cur = rows_of(vc_ref, r, start, ATTN_BLOCK).astype(BF16)
            q2 = rows_of(q_ref, r, start, ATTN_BLOCK).astype(BF16)
            o, l = _attend_pair(q2, jnp.concatenate([k_prev, k_cur], axis=0), jnp.concatenate([v_prev, v_cur], axis=0),
                                bias_head if j == 0 else bias_any, None)
            if dil == 1:
                o_ref[start:start + ATTN_BLOCK, :] = o
                l_ref[start:start + ATTN_BLOCK, :] = l
            else:
                o_ref[pl.ds(start * dil + r, ATTN_BLOCK, stride=dil), :] = o
                l_ref[pl.ds(start * dil + r, ATTN_BLOCK, stride=dil), :] = l
            k_prev, v_prev = k_cur, v_cur


def _attn_tile(length):
    return min(512, length)


def _swa_attention(qkv, sinks, batch, seq):
    tq = _attn_tile(seq)
    qkv = qkv.reshape(batch, seq, A_Q + 2 * A_KV_DUP)
    k_blk = A_Q // A_KV_DUP
    v_blk = k_blk + 1
    sub = tq // ATTN_BLOCK

    def prev(n):
        return jnp.maximum(n * sub - 1, 0)

    kern = functools.partial(_swa_attn_kernel, pairs_per_kv=SWA_Q_HEADS // SWA_KV_HEADS // 2, tq=tq)
    return pl.pallas_call(
        kern,
        grid=(batch, seq // tq),
        in_specs=[
            pl.BlockSpec(memory_space=pltpu.SMEM),
            pl.BlockSpec((None, tq, A_Q), lambda b, n: (b, n, 0)),
            pl.BlockSpec((None, tq, A_KV_DUP), lambda b, n: (b, n, k_blk)),
            pl.BlockSpec((None, ATTN_BLOCK, A_KV_DUP), lambda b, n: (b, prev(n), k_blk)),
            pl.BlockSpec((None, tq, A_KV_DUP), lambda b, n: (b, n, v_blk)),
            pl.BlockSpec((None, ATTN_BLOCK, A_KV_DUP), lambda b, n: (b, prev(n), v_blk)),
        ],
        out_specs=pl.BlockSpec((None, tq, A_Q), lambda b, n: (b, n, 0)),
        out_shape=jax.ShapeDtypeStruct((batch, seq, A_Q), BF16),
        scratch_shapes=[pltpu.VMEM((tq + ATTN_BLOCK, A_KV_DUP), BF16), pltpu.VMEM((tq + ATTN_BLOCK, A_KV_DUP), BF16)],
        compiler_params=_cparams("arbitrary", "arbitrary"),
        name="swa_attn",
    )(sinks, qkv, qkv, qkv, qkv, qkv).reshape(batch * seq, A_Q)


def _dilated_attention(qk, v, batch, seq, gi):
    window, dil = DIL_CONFIGS[gi]
    back = ATTN_BLOCK * dil
    rows = max(_attn_tile(seq), back)
    assert seq % rows == 0 and rows % back == 0
    pairs = DIL_HEADS_PER_GROUP // 2
    q_blk, k_blk, v_blk = gi * pairs, (C_QKV // LANES) + gi * pairs, gi * pairs
    per_seq = seq // rows
    ratio = rows // back

    def cur(col0):
        return pl.BlockSpec((rows, LANES), lambda b, n, hp: (b * per_seq + n, col0 + hp))

    def prev(col0):
        return pl.BlockSpec((back, LANES), lambda b, n, hp: (b * per_seq * ratio + jnp.maximum(n * ratio - 1, 0), col0 + hp))

    out_sds = jax.ShapeDtypeStruct((batch * seq, C_OUT), F32)
    out_spec = pl.BlockSpec((rows, LANES), lambda b, n, hp: (b * per_seq + n, hp))
    return pl.pallas_call(
        functools.partial(_dil_attn_kernel, dil=dil, max_dist=window // dil),
        grid=(batch, per_seq, pairs),
        in_specs=[cur(q_blk), cur(k_blk), prev(k_blk), cur(v_blk), prev(v_blk)],
        out_specs=[out_spec, out_spec],
        out_shape=[out_sds, out_sds],
        compiler_params=_cparams("arbitrary", "arbitrary", "arbitrary"),
        name=f"dil_attn_{dil}",
    )(qk, qk, qk, v, v)


def _ssd_kernel(z_ref, xs_ref, bc_ref, dt_ref, cwx_ref, cwbc_ref, cbx_ref, cbbc_ref, dtb_ref, alog_ref,
                dsk_ref, nw_ref, o_ref, extx, extbc, hst):
    q = SSD_CHUNK
    c = pl.program_id(1)

    @pl.when(c == 0)
    def _():
        extx[0:SUBLANES, :] = jnp.zeros((SUBLANES, SSD_INNER), F32)
        extbc[0:SUBLANES, :] = jnp.zeros((SUBLANES, SSD_BC_WIDTH), F32)
        hst[...] = jnp.zeros(hst.shape, F32)

    extx[SUBLANES:, :] = xs_ref[...]
    extbc[SUBLANES:, :] = bc_ref[...]

    def conv_silu(ext, cw_ref, cb_ref):
        acc = cb_ref[...]
        for k in range(SSD_CONV):
            start = SUBLANES - (SSD_CONV - 1) + k
            acc = acc + cw_ref[k:k + 1, :] * ext[start:start + q, :]
        return acc * _sigmoid(acc)

    xa = conv_silu(extx, cwx_ref, cbx_ref)
    bca = conv_silu(extbc, cwbc_ref, cbbc_ref)
    extx[0:SUBLANES, :] = xs_ref[q - SUBLANES:q, :]
    extbc[0:SUBLANES, :] = bc_ref[q - SUBLANES:q, :]

    dtv = dt_ref[...] + dtb_ref[...]
    dt = jnp.maximum(dtv, 0.0) + jnp.log(1.0 + jnp.exp(-jnp.abs(dtv)))
    a = -jnp.exp(alog_ref[...])
    rr = lax.broadcasted_iota(I32, (q, q), 0)
    cc = lax.broadcasted_iota(I32, (q, q), 1)
    causal = rr >= cc
    tril = jnp.where(causal, 1.0, 0.0).astype(F32)
    acs = jnp.dot(tril, dt * a, preferred_element_type=F32, precision=lax.Precision.HIGHEST)
    acs_t = acs.T

    low = lax.broadcasted_iota(I32, (q, LANES), 1) < SSD_HEADDIM
    low_state = lax.broadcasted_iota(I32, (SSD_STATE, LANES), 1) < SSD_HEADDIM
    pairs_per_group = SSD_HEADS_PER_GROUP // 2

    def per_head(h0, fn):
        return jnp.where(low, fn(h0), fn(h0 + 1))

    for g in range(SSD_GROUPS):
        bg = bca[:, g * SSD_STATE:(g + 1) * SSD_STATE]
        cg = bca[:, (SSD_GROUPS + g) * SSD_STATE:(SSD_GROUPS + g + 1) * SSD_STATE].astype(BF16)
        bt = bg.T.astype(BF16)
        cb = jnp.dot(cg, bt, preferred_element_type=F32)
        ys = []
        for pi in range(pairs_per_group):
            pair = g * pairs_per_group + pi
            h0 = 2 * pair
            cols = slice(pair * LANES, (pair + 1) * LANES)
            xp = xa[:, cols]
            xdt = xp * per_head(h0, lambda h: dt[:, h:h + 1])
            m_both = jnp.concatenate(
                [(cb * jnp.exp(jnp.where(causal, acs[:, h:h + 1] - acs_t[h:h + 1, :], NEG_INF))).astype(BF16)
                 for h in (h0, h0 + 1)], axis=1)
            x_both = jnp.concatenate([jnp.where(low, xdt, 0.0), jnp.where(low, 0.0, xdt)], axis=0).astype(BF16)
            y = jnp.dot(m_both, x_both, preferred_element_type=F32)
            hprev = hst[pair]
            y = y + jnp.dot(cg, hprev.astype(BF16), preferred_element_type=F32) * per_head(h0, lambda h: jnp.exp(acs[:, h:h + 1]))
            y = y + xp * dsk_ref[:, cols]
            to_end = per_head(h0, lambda h: jnp.exp(acs[q - 1:q, h:h + 1] - acs[:, h:h + 1]))
            chunk_decay = jnp.where(low_state, jnp.exp(acs[q - 1:q, h0:h0 + 1]), jnp.exp(acs[q - 1:q, h0 + 1:h0 + 2]))
            hst[pair] = hprev * chunk_decay + jnp.dot(bt, (xdt * to_end).astype(BF16), preferred_element_type=F32)
            ys.append(y)
        lo, hi = g * SSD_GROUP_WIDTH, (g + 1) * SSD_GROUP_WIDTH
        zg = z_ref[:, lo:hi]
        yg = jnp.concatenate(ys, axis=1) * (zg * _sigmoid(zg))
        yg = yg * lax.rsqrt(jnp.mean(yg * yg, axis=-1, keepdims=True) + LN_EPS)
        o_ref[:, lo:hi] = (yg * nw_ref[:, lo:hi]).astype(o_ref.dtype)


def _ssd_mixer(misc, conv_w, conv_b, dt_bias, a_log, d_skip, norm_w, batch, seq):
    q = SSD_CHUNK
    nc = seq // q
    pad = LANES - SSD_HEADS
    row = lambda v: v.reshape(1, -1)
    args = [
        misc, misc, misc, misc,
        conv_w[:, :SSD_INNER], conv_w[:, SSD_INNER:], row(conv_b[:SSD_INNER]), row(conv_b[SSD_INNER:]),
        row(jnp.pad(dt_bias, (0, pad))), row(jnp.pad(a_log, (0, pad))),
        row(jnp.repeat(d_skip, SSD_HEADDIM)), row(norm_w),
    ]
    tok = lambda b, c: b * nc + c
    full = lambda shape: pl.BlockSpec(shape, lambda b, c: (0, 0))
    bc_blk = 2 * SSD_INNER // SSD_BC_WIDTH
    dt_blk = (2 * SSD_INNER + SSD_BC_WIDTH) // LANES
    return pl.pallas_call(
        _ssd_kernel,
        grid=(batch, nc),
        in_specs=[
            pl.BlockSpec((q, SSD_INNER), lambda b, c: (tok(b, c), 0)),
            pl.BlockSpec((q, SSD_INNER), lambda b, c: (tok(b, c), 1)),
            pl.BlockSpec((q, SSD_BC_WIDTH), lambda b, c: (tok(b, c), bc_blk)),
            pl.BlockSpec((q, LANES), lambda b, c: (tok(b, c), dt_blk)),
            full((SSD_CONV, SSD_INNER)), full((SSD_CONV, SSD_BC_WIDTH)), full((1, SSD_INNER)), full((1, SSD_BC_WIDTH)),
            full((1, LANES)), full((1, LANES)), full((1, SSD_INNER)), full((1, SSD_INNER)),
        ],
        out_specs=pl.BlockSpec((q, SSD_INNER), lambda b, c: (tok(b, c), 0)),
        out_shape=jax.ShapeDtypeStruct((batch * seq, SSD_INNER), BF16),
        scratch_shapes=[
            pltpu.VMEM((SUBLANES + q, SSD_INNER), F32),
            pltpu.VMEM((SUBLANES + q, SSD_BC_WIDTH), F32),
            pltpu.VMEM((SSD_HEADS // 2, SSD_STATE, LANES), F32),
        ],
        compiler_params=_cparams("arbitrary", "arbitrary"),
        name="ssd_mixer",
    )(*args)


def _merge_kernel(ysw_ref, yss_ref, o0_ref, l0_ref, o1_ref, l1_ref, o2_ref, l2_ref, g0_ref, g1_ref, g2_ref,
                  p0_ref, p1_ref, p2_ref, out_ref):
    l0, l1, l2 = l0_ref[...], l1_ref[...], l2_ref[...]
    m = jnp.maximum(jnp.maximum(l0, l1), l2)
    e0, e1, e2 = jnp.exp(l0 - m), jnp.exp(l1 - m), jnp.exp(l2 - m)
    inv = 1.0 / (e0 + e1 + e2)
    ydil = (o0_ref[...] * (e0 * inv) + o1_ref[...] * (e1 * inv) + o2_ref[...] * (e2 * inv)).astype(BF16)
    merged = g0_ref[...] * jnp.dot(ysw_ref[...], p0_ref[...], preferred_element_type=F32)
    merged = merged + g1_ref[...] * jnp.dot(yss_ref[...], p1_ref[...], preferred_element_type=F32)
    merged = merged + g2_ref[...] * jnp.dot(ydil, p2_ref[...], preferred_element_type=F32)
    out_ref[...] = merged.astype(out_ref.dtype)


def _merge(y_swa, y_ssd, dil, gates, p_swa, p_ssd, p_dil, tm, tn):
    t = y_swa.shape[0]
    d = p_swa.shape[1]
    nj = d // tn
    rows = lambda width: pl.BlockSpec((tm, width), lambda i, j: (i, 0))
    gate = lambda k: pl.BlockSpec((tm, tn), lambda i, j: (i, k * nj + j))
    wcol = lambda width: pl.BlockSpec((width, tn), lambda i, j: (0, j))
    dil_args = [a for pair in dil for a in pair]
    return pl.pallas_call(
        _merge_kernel,
        grid=(t // tm, nj),
        in_specs=[rows(A_Q), rows(SSD_INNER)] + [rows(C_OUT)] * 6 + [gate(0), gate(1), gate(2)]
        + [wcol(A_Q), wcol(SSD_INNER), wcol(C_OUT)],
        out_specs=pl.BlockSpec((tm, tn), lambda i, j: (i, j)),
        out_shape=jax.ShapeDtypeStruct((t, d), BF16),
        compiler_params=_cparams("arbitrary", "arbitrary"),
        name="merge",
    )(y_swa, y_ssd, *dil_args, gates, gates, gates, p_swa, p_ssd, p_dil)


def _layer_norm(h, g, b):
    mu = jnp.mean(h, axis=-1, keepdims=True)
    hc = h - mu
    var = jnp.mean(hc * hc, axis=-1, keepdims=True)
    return hc * lax.rsqrt(var + LN_EPS) * g + b


LN1_ROW_SPLIT = 2


def _ln1_router_kernel(x_ref, m_ref, w_ref, g_ref, b_ref, wr_ref, br_ref, xo_ref, ids_ref, comb_ref, cnt_ref, *, alpha):
    i = pl.program_id(0)
    tm = x_ref.shape[0]
    rows = tm // LN1_ROW_SPLIT
    lane = lax.broadcasted_iota(I32, (rows, LANES), 1).astype(F32)
    big = float(2 * LANES)

    def first_argmax(vals):
        top = jnp.max(vals, axis=-1, keepdims=True)
        return top, jnp.min(jnp.where(vals == top, lane, big), axis=-1, keepdims=True)

    tops = []
    for part in range(LN1_ROW_SPLIT):
        sl = slice(part * rows, (part + 1) * rows)
        h = alpha * x_ref[sl, :] + jnp.dot(m_ref[sl, :], w_ref[...], preferred_element_type=F32)
        y = _layer_norm(h, g_ref[...], b_ref[...])
        xo_ref[sl, :] = y

        logits = jnp.dot(y.astype(BF16), wr_ref[...], preferred_element_type=F32) + br_ref[...]
        gl = jnp.where(lane < N_GROUPS, logits, NEG_INF)
        gmax, gsel = first_argmax(gl)
        p_group = 1.0 / jnp.sum(jnp.exp(gl - gmax), axis=-1, keepdims=True)
        lo = N_GROUPS + gsel * EXPERTS_PER_GROUP
        el = jnp.where((lane >= lo) & (lane < lo + EXPERTS_PER_GROUP), logits, NEG_INF)
        v1, i1 = first_argmax(el)
        v2, i2 = first_argmax(jnp.where(lane == i1, NEG_INF, el))
        e21 = jnp.exp(v2 - v1)
        inv = 1.0 / (1.0 + e21)
        comb_ref[sl, :] = jnp.where(lane == 0.0, inv * p_group, jnp.where(lane == 1.0, e21 * inv * p_group, 0.0))
        tops.append((i1, i2))

    @pl.when(i == 0)
    def _():
        cnt_ref[...] = jnp.zeros(cnt_ref.shape, F32)

    i1 = jnp.concatenate([t[0] for t in tops], axis=0)
    i2 = jnp.concatenate([t[1] for t in tops], axis=0)
    lane = lax.broadcasted_iota(I32, (tm, LANES), 1).astype(F32)
    pick1 = jnp.where(lane == i1, 1.0, 0.0)
    pick2 = jnp.where(lane == i2, 1.0, 0.0)
    picks = pick1 + pick2
    earlier = lax.broadcasted_iota(I32, (tm, tm), 0) > lax.broadcasted_iota(I32, (tm, tm), 1)
    before = jnp.dot(jnp.where(earlier, 1.0, 0.0).astype(BF16), picks.astype(BF16), preferred_element_type=F32) + cnt_ref[...]
    rank1 = jnp.sum(pick1 * before, axis=-1, keepdims=True)
    rank2 = jnp.sum(pick2 * before, axis=-1, keepdims=True)
    cnt_ref[...] = cnt_ref[...] + jnp.sum(picks, axis=0, keepdims=True)
    ids = jnp.where(lane == 0.0, i1 - N_GROUPS, jnp.where(lane == 1.0, i2 - N_GROUPS,
                    jnp.where(lane == 2.0, rank1, jnp.where(lane == 3.0, rank2, 0.0))))
    for c in range(tm // LANES):
        ids_ref[:, c * LANES:(c + 1) * LANES] = ids[c * LANES:(c + 1) * LANES, :].T[0:SUBLANES, :].astype(I32)


def _ln1_router(x, merged, w_out, g, b, wr, br, alpha, tm):
    t, d = x.shape
    rows = lambda width: pl.BlockSpec((tm, width), lambda i: (i, 0))
    full = lambda shape: pl.BlockSpec(shape, lambda i: (0, 0))
    once = lambda shape: pl.BlockSpec(shape, lambda i: (0, 0), pipeline_mode=pl.Buffered(1))
    return pl.pallas_call(
        functools.partial(_ln1_router_kernel, alpha=alpha),
        grid=(t // tm,),
        in_specs=[rows(d), rows(d), once((d, d)), full((1, d)), full((1, d)), once((d, LANES)), full((1, LANES))],
        out_specs=[rows(d), pl.BlockSpec((SUBLANES, tm), lambda i: (0, i)), rows(LANES), full((1, LANES))],
        out_shape=[jax.ShapeDtypeStruct((t, d), F32),
                   jax.ShapeDtypeStruct((SUBLANES, t), I32), jax.ShapeDtypeStruct((t, LANES), F32),
                   jax.ShapeDtypeStruct((1, LANES), F32)],
        compiler_params=_cparams("arbitrary"),
        name="ln1_router",
    )(x, merged, w_out, g.reshape(1, d), b.reshape(1, d), wr, br)


MOE_GATHER_AHEAD = 2
MOE_X_BUFFERS = 4


def _moe_kernel(be_ref, dest_ref, pad_lo_ref, pad_hi_ref, nu_ref, x_hbm, w1_ref, w3_ref, w2_ref, y_hbm,
                xbuf0, xbuf1, xbuf2, xbuf3, ybuf0, ybuf1, w1b, w3b, w2b, tok_ref, dst_ref, gsem, ssem, *, n_tok):
    b = pl.program_id(0)
    n_used = nu_ref[0]
    xbufs, ybufs = (xbuf0, xbuf1, xbuf2, xbuf3), (ybuf0, ybuf1)
    trash_row0 = TOP_K * n_tok

    def entry(blk, r):
        return (blk + 1) * MOE_BLOCK + r

    def gather_copy(blk, p, r):
        return pltpu.make_async_copy(
            x_hbm.at[pl.ds(tok_ref[entry(blk, r)], 1), :], xbufs[p].at[pl.ds(r, 1), :], gsem.at[p])

    def scatter_copy(blk, p, r):
        return pltpu.make_async_copy(
            ybufs[p].at[pl.ds(r, 1), :], y_hbm.at[pl.ds(dst_ref[entry(blk, r)], 1), :], ssem.at[p])

    def for_rows(fn):
        def chunk(c, carry):
            for u in range(MOE_DMA_UNROLL):
                fn(c * MOE_DMA_UNROLL + u)
            return carry
        lax.fori_loop(0, MOE_BLOCK // MOE_DMA_UNROLL, chunk, 0)

    @pl.when(b == 0)
    def _():
        def pad_entries(lo, hi):
            def pad_entry(s, carry):
                tok_ref[s] = 0
                dst_ref[s] = trash_row0 + lax.bitwise_and(s, MOE_BLOCK - 1)
                return carry
            lax.fori_loop(lo, hi, pad_entry, 0)

        pad_entries(0, MOE_BLOCK)
        lax.fori_loop(0, N_EXPERTS, lambda e, c: (pad_entries(entry(0, pad_lo_ref[e]), entry(0, pad_hi_ref[e])), c)[1], 0)
        pad_entries(entry(n_used, 0), entry(n_used + MOE_GATHER_AHEAD, 0))

        for k in range(TOP_K):
            def fill(c, carry):
                for u in range(MOE_DMA_UNROLL):
                    tkn = c * MOE_DMA_UNROLL + u
                    s = dest_ref[k * n_tok + tkn] + MOE_BLOCK
                    tok_ref[s] = tkn
                    dst_ref[s] = k * n_tok + tkn
                return carry
            lax.fori_loop(0, n_tok // MOE_DMA_UNROLL, fill, 0)

        ybuf1[...] = jnp.zeros(ybuf1.shape, F32)
        for ahead in range(MOE_GATHER_AHEAD):
            for_rows(lambda r: gather_copy(ahead, ahead, r).start())

    def step(p):
        py = p % 2
        qy = 1 - py
        px_next = (p + MOE_GATHER_AHEAD) % MOE_X_BUFFERS
        for_rows(lambda r: gather_copy(b, p, r).wait())

        @pl.when(b >= 1)
        def _():
            for_rows(lambda r: scatter_copy(b - 2, py, r).wait())

        @pl.when(jnp.logical_or(b == 0, be_ref[b] != be_ref[jnp.maximum(b - 1, 0)]))
        def _():
            w1b[...] = w1_ref[...].astype(BF16)
            w3b[...] = w3_ref[...].astype(BF16)
            w2b[...] = w2_ref[...].astype(BF16)

        def issue(part):
            def row(r, carry):
                gather_copy(b + MOE_GATHER_AHEAD, px_next, r).start()
                scatter_copy(b - 1, qy, r).start()
                return carry
            lax.fori_loop(part * MOE_BLOCK // 4, (part + 1) * MOE_BLOCK // 4, row, 0, unroll=True)

        xb = xbufs[p][...].astype(BF16)
        issue(0)
        h1 = jnp.dot(xb, w1b[...], preferred_element_type=F32)
        issue(1)
        h3 = jnp.dot(xb, w3b[...], preferred_element_type=F32)
        issue(2)
        h = (h1 * _sigmoid(h1) * h3).astype(BF16)
        issue(3)
        ybufs[py][...] = jnp.dot(h, w2b[...], preferred_element_type=F32)

        @pl.when(b == n_used - 1)
        def _():
            for_rows(lambda r: scatter_copy(b - 1, qy, r).wait())
            for_rows(lambda r: scatter_copy(b, py, r).start())
            for_rows(lambda r: scatter_copy(b, py, r).wait())
            for ahead in range(1, MOE_GATHER_AHEAD + 1):
                for_rows(lambda r: gather_copy(b + ahead, (p + ahead) % MOE_X_BUFFERS, r).wait())

    for p in range(MOE_X_BUFFERS):
        pl.when(jnp.logical_and(b < n_used, b % MOE_X_BUFFERS == p))(functools.partial(step, p))


def _moe_experts(x1, plan, w1, w3, w2, layer):
    t, d = x1.shape
    block_e, dest, pad_lo, pad_hi, n_used = plan
    n_blocks = block_e.shape[0]
    table = pltpu.SMEM(((n_blocks + 1 + MOE_GATHER_AHEAD) * MOE_BLOCK,), I32)
    row_buf = pltpu.VMEM((MOE_BLOCK, d), F32)
    wspec = lambda shape: pl.BlockSpec((None, None) + shape, lambda b, be, *_: (layer, be[b], 0, 0))
    grid_spec = pltpu.PrefetchScalarGridSpec(
        num_scalar_prefetch=5,
        grid=(n_blocks,),
        in_specs=[pl.BlockSpec(memory_space=pl.ANY), wspec((d, D_EXPERT)), wspec((d, D_EXPERT)), wspec((D_EXPERT, d))],
        out_specs=pl.BlockSpec(memory_space=pl.ANY),
        scratch_shapes=[row_buf] * (MOE_X_BUFFERS + 2) + [
            pltpu.VMEM((d, D_EXPERT), BF16), pltpu.VMEM((d, D_EXPERT), BF16), pltpu.VMEM((D_EXPERT, d), BF16),
            table, table,
            pltpu.SemaphoreType.DMA((MOE_X_BUFFERS,)),
            pltpu.SemaphoreType.DMA((2,)),
        ],
    )
    return pl.pallas_call(
        functools.partial(_moe_kernel, n_tok=t),
        grid_spec=grid_spec,
        out_shape=jax.ShapeDtypeStruct((TOP_K * t + MOE_BLOCK, d), F32),
        compiler_params=_cparams("arbitrary"),
        name="moe_experts",
    )(block_e, dest, pad_lo, pad_hi, n_used, x1, w1, w3, w2)


def _ln2_kernel(x_ref, y0_ref, y1_ref, comb_ref, g_ref, b_ref, xo_ref, xb_ref, *, alpha):
    comb = comb_ref[...]
    h = alpha * x_ref[...] + comb[:, 0:1] * y0_ref[...] + comb[:, 1:2] * y1_ref[...]
    y = _layer_norm(h, g_ref[...], b_ref[...])
    xo_ref[...] = y
    xb_ref[...] = y.astype(BF16)


def _ln2(x1, y2, comb, g, b, alpha, tm):
    t, d = x1.shape
    nblk = t // tm
    rows = lambda width: pl.BlockSpec((tm, width), lambda i: (i, 0))
    full = lambda shape: pl.BlockSpec(shape, lambda i: (0, 0))
    return pl.pallas_call(
        functools.partial(_ln2_kernel, alpha=alpha),
        grid=(nblk,),
        in_specs=[rows(d), rows(d), pl.BlockSpec((tm, d), lambda i: (nblk + i, 0)), rows(LANES), full((1, d)), full((1, d))],
        out_specs=[rows(d), rows(d)],
        out_shape=[jax.ShapeDtypeStruct((t, d), F32), jax.ShapeDtypeStruct((t, d), BF16)],
        compiler_params=_cparams("arbitrary"),
        name="ln2",
    )(x1, y2, y2, comb, g.reshape(1, d), b.reshape(1, d))


def _dispatch_plan(ids, counts, n_tok):
    n_assign = n_tok * TOP_K
    n_blocks = -(-n_assign // MOE_BLOCK) + N_EXPERTS
    experts = jnp.arange(N_EXPERTS, dtype=I32)
    padded = (counts + MOE_BLOCK - 1) // MOE_BLOCK * MOE_BLOCK
    pad_end = jnp.cumsum(padded)
    pad_start = pad_end - padded
    slot0 = jnp.sum(jnp.where(ids[0:TOP_K, :, None] == experts, pad_start, 0), axis=-1)
    dest = (slot0 + ids[TOP_K:2 * TOP_K]).reshape(n_assign)
    starts = jnp.arange(n_blocks, dtype=I32) * MOE_BLOCK
    block_e = jnp.minimum(jnp.sum((pad_end[None, :] <= starts[:, None]).astype(I32), axis=1), N_EXPERTS - 1)
    return block_e, dest, pad_start + counts, pad_end, pad_end[-1:] // MOE_BLOCK


def _rope_tables(seq):
    inv_freq = 1.0 / (ROPE_THETA ** (jnp.arange(0, HEAD_DIM, 2, dtype=F32) / HEAD_DIM))
    ang = jnp.arange(seq, dtype=F32)[:, None] * inv_freq[None, :]
    cos, sin = jnp.cos(ang), jnp.sin(ang)
    reps = LANES // HEAD_DIM
    return jnp.tile(jnp.concatenate([cos, cos], axis=1), (1, reps)), jnp.tile(jnp.concatenate([-sin, sin], axis=1), (1, reps))


def _tile(total, want):
    return min(total, want)


def kernel(x, w_in, b_in, attn_sinks, conv_w, conv_b, dt_bias, a_log, d_skip, ssd_norm_w, proj_swa, proj_ssd, proj_dil, w_out, ln1_g, ln1_b, router_group_w, router_group_b, router_expert_w, router_expert_b, expert_w1, expert_w3, expert_w2, ln2_g, ln2_b):
    batch, seq, d = x.shape
    depth = w_in.shape[0]
    t = batch * seq
    alpha = (2 * depth) ** 0.25
    rope = _rope_tables(seq)
    tm = _tile(seq, 512)
    tmp = _tile(seq, 1024)

    sizes = (A_Q, A_KV, A_KV, SSD_INNER, SSD_INNER + SSD_BC_WIDTH, SSD_HEADS, C_QKV, C_QKV, C_QKV, 3 * d)
    offs = [0]
    for s in sizes:
        offs.append(offs[-1] + s)
    o_qa, o_ka, o_va, o_z, o_xbc, o_dt, o_qc, o_kc, o_vc, o_gate, o_end = offs

    n_a, n_misc, n_gate = o_z - o_qa, o_qc - o_z, o_end - o_gate
    dt_pad = LANES - SSD_HEADS

    def piece(lo, hi, pad=0):
        wp, bp = w_in[:, :, lo:hi], b_in[:, None, lo:hi]
        if pad:
            wp, bp = jnp.pad(wp, ((0, 0), (0, 0), (0, pad))), jnp.pad(bp, ((0, 0), (0, 0), (0, pad)))
        return wp.astype(BF16), bp

    w_a, b_a = piece(o_qa, o_z)
    w_misc, b_misc = piece(o_z, o_qc, dt_pad)
    w_c, b_c = piece(o_qc, o_gate)
    w_g, b_g = piece(o_gate, o_end)
    p_swa, p_ssd, p_dil, w_o = (m.astype(BF16) for m in (proj_swa, proj_ssd, proj_dil, w_out))
    wr_all = jnp.concatenate(
        [router_group_w, router_expert_w.transpose(0, 2, 1, 3).reshape(depth, d, N_EXPERTS)], axis=2)
    wr_all = jnp.pad(wr_all, ((0, 0), (0, 0), (0, LANES - N_GROUPS - N_EXPERTS))).astype(BF16)
    br_all = jnp.pad(jnp.concatenate([router_group_b, router_expert_b.reshape(depth, N_EXPERTS)], axis=1),
                     ((0, 0), (0, LANES - N_GROUPS - N_EXPERTS)))

    xf = x.reshape(t, d)
    xb = xf.astype(BF16)
    for l in range(depth):
        qkv_a = _proj(xb, w_a, b_a, l, 0, n_a, BF16, tmp, n_a, seq, rope=rope, rope_blocks=(A_Q + A_KV) // LANES,
                      dup_from=A_Q // LANES, name="proj_swa_qkv")
        qk_c = _proj(xb, w_c, b_c, l, 0, 2 * C_QKV, F32, tmp, C_QKV, seq, rope=rope, rope_blocks=C_QKV // LANES, name="proj_dil_qk")
        v_c = _proj(xb, w_c, b_c, l, 2 * C_QKV, C_QKV, F32, tmp, C_QKV, seq, name="proj_dil_v")
        misc = _proj(xb, w_misc, b_misc, l, 0, n_misc + dt_pad, F32, tmp, (n_misc + dt_pad) // 3, seq, name="proj_ssd")
        gates = _proj(xb, w_g, b_g, l, 0, n_gate, F32, tmp, 1024, seq, act="sigmoid", name="proj_gates")

        y_swa = _swa_attention(qkv_a, attn_sinks[l], batch, seq)
        y_ssd = _ssd_mixer(misc, conv_w[l], conv_b[l], dt_bias[l], a_log[l], d_skip[l], ssd_norm_w[l], batch, seq)
        dil = [_dilated_attention(qk_c, v_c, batch, seq, gi) for gi in range(len(DIL_CONFIGS))]

        merged = _merge(y_swa, y_ssd, dil, gates, p_swa[l], p_ssd[l], p_dil[l], tm, 1024)
        x1, ids, comb, cnt = _ln1_router(xf, merged, w_o[l], ln1_g[l], ln1_b[l], wr_all[l], br_all[l].reshape(1, LANES), alpha, tm)

        plan = _dispatch_plan(ids, cnt[0, N_GROUPS:N_GROUPS + N_EXPERTS].astype(I32), t)
        y2 = _moe_experts(x1, plan, expert_w1, expert_w3, expert_w2, l)
        xf, xb = _ln2(x1, y2, comb, ln2_g[l], ln2_b[l], alpha, tm)
    return xf.reshape(batch, seq, d)
```

```python
import functools

import jax
import jax.numpy as jnp
from jax import lax
from jax.experimental import pallas as pl
from jax.experimental.pallas import tpu as pltpu

F32 = jnp.float32
BF16 = jnp.bfloat16
I32 = jnp.int32

LANES = 128
SUBLANES = 8
VMEM_LIMIT_BYTES = 48 * 1024 * 1024

HEAD_DIM = 64
HALF_HEAD = HEAD_DIM // 2
ROPE_THETA = 10000.0
ATTN_BLOCK = 128
LN_EPS = 1e-5

SWA_Q_HEADS = 16
SWA_KV_HEADS = 2
SWA_WINDOW = 128

SSD_HEADS = 24
SSD_HEADDIM = 64
SSD_INNER = SSD_HEADS * SSD_HEADDIM
SSD_GROUPS = 4
SSD_STATE = 128
SSD_CONV = 4
SSD_CHUNK = 128
SSD_HEADS_PER_GROUP = SSD_HEADS // SSD_GROUPS
SSD_GROUP_WIDTH = SSD_INNER // SSD_GROUPS
SSD_BC_WIDTH = 2 * SSD_GROUPS * SSD_STATE

DIL_CONFIGS = ((128, 1), (512, 4), (2048, 16))
DIL_HEADS_PER_GROUP = 6
DIL_HEADS = DIL_HEADS_PER_GROUP * len(DIL_CONFIGS)

A_Q = SWA_Q_HEADS * HEAD_DIM
A_KV = SWA_KV_HEADS * HEAD_DIM
A_KV_DUP = 2 * A_KV
C_QKV = DIL_HEADS * HEAD_DIM
C_OUT = DIL_HEADS_PER_GROUP * HEAD_DIM

N_GROUPS = 4
EXPERTS_PER_GROUP = 8
N_EXPERTS = N_GROUPS * EXPERTS_PER_GROUP
TOP_K = 2
D_EXPERT = 512
MOE_BLOCK = 128
MOE_DMA_UNROLL = 8

NEG_INF = float("-inf")


def _cparams(*sem):
    return pltpu.CompilerParams(dimension_semantics=sem, vmem_limit_bytes=VMEM_LIMIT_BYTES)


def _sigmoid(v):
    return 1.0 / (1.0 + jnp.exp(-v))


def _proj_kernel(*refs, rope_blocks, dup_from, act):
    if rope_blocks:
        x_ref, w_ref, b_ref, cos_ref, sin_ref, o_ref = refs
    else:
        x_ref, w_ref, b_ref, o_ref = refs
    acc = jnp.dot(x_ref[...], w_ref[...], preferred_element_type=F32) + b_ref[...]
    n_blocks = acc.shape[1] // LANES
    if rope_blocks or dup_from < n_blocks:
        lane = lax.broadcasted_iota(I32, (acc.shape[0], LANES), 1)
        first_half = (lane % HEAD_DIM) < HALF_HEAD
        low_head = lane < HEAD_DIM
        out_c = 0
        for c in range(n_blocks):
            v = acc[:, c * LANES:(c + 1) * LANES]
            if c < rope_blocks:
                partner = jnp.where(first_half, pltpu.roll(v, LANES - HALF_HEAD, 1), pltpu.roll(v, HALF_HEAD, 1))
                v = v * cos_ref[...] + partner * sin_ref[...]
            if c >= dup_from:
                other = pltpu.roll(v, HEAD_DIM, 1)
                outs = (jnp.where(low_head, v, other), jnp.where(low_head, other, v))
            else:
                outs = (v,)
            for o in outs:
                o_ref[:, out_c * LANES:(out_c + 1) * LANES] = o.astype(o_ref.dtype)
                out_c += 1
    elif act == "sigmoid":
        o_ref[...] = _sigmoid(acc).astype(o_ref.dtype)
    else:
        o_ref[...] = acc.astype(o_ref.dtype)


def _proj(xb, w, b, layer, col0, n, out_dtype, tm, tn, seq, rope=None, rope_blocks=0, dup_from=None, act=None, name="proj"):
    t, d = xb.shape
    assert t % tm == 0 and n % tn == 0 and seq % tm == 0 and col0 % tn == 0
    j0 = col0 // tn
    in_blocks = tn // LANES
    if dup_from is None:
        dup_from = in_blocks
    assert dup_from == in_blocks or n == tn
    tn_out = tn + (in_blocks - dup_from) * LANES
    in_specs = [
        pl.BlockSpec((tm, d), lambda j, i: (i, 0)),
        pl.BlockSpec((None, d, tn), lambda j, i: (layer, 0, j0 + j)),
        pl.BlockSpec((None, 1, tn), lambda j, i: (layer, 0, j0 + j)),
    ]
    args = [xb, w, b]
    if rope_blocks:
        sblocks = seq // tm
        in_specs += [pl.BlockSpec((tm, LANES), lambda j, i: (i % sblocks, 0))] * 2
        args += list(rope)
    return pl.pallas_call(
        functools.partial(_proj_kernel, rope_blocks=rope_blocks, dup_from=dup_from, act=act),
        grid=(n // tn, t // tm),
        in_specs=in_specs,
        out_specs=pl.BlockSpec((tm, tn_out), lambda j, i: (i, j)),
        out_shape=jax.ShapeDtypeStruct((t, n // tn * tn_out), out_dtype),
        compiler_params=_cparams("arbitrary", "arbitrary"),
        name=name,
    )(*args)


def _band_biases(max_dist):
    row = lax.broadcasted_iota(I32, (ATTN_BLOCK, 2 * ATTN_BLOCK), 0)
    col = lax.broadcasted_iota(I32, (ATTN_BLOCK, 2 * ATTN_BLOCK), 1)
    dist = row + ATTN_BLOCK - col
    in_band = (dist >= 0) & (dist <= max_dist)
    bias_any = jnp.where(in_band, 0.0, NEG_INF).astype(F32)
    bias_first = jnp.where(in_band & (col >= ATTN_BLOCK), 0.0, NEG_INF).astype(F32)
    return bias_any, bias_first


def _attend_pair(q2, k2, v2, bias, sinks):
    lane = lax.broadcasted_iota(I32, (ATTN_BLOCK, LANES), 1)
    low = lane < HEAD_DIM
    scale = HEAD_DIM ** -0.5
    outs, lses = [], []
    for half in range(2):
        qh = jnp.where(low if half == 0 else jnp.logical_not(low), q2, jnp.zeros_like(q2))
        s = lax.dot_general(qh, k2, (((1,), (1,)), ((), ())), preferred_element_type=F32) * scale
        s = s + bias
        m = jnp.max(s, axis=-1, keepdims=True)
        if sinks is not None:
            m = jnp.maximum(m, sinks[half])
        p = jnp.exp(s - m)
        den = jnp.sum(p, axis=-1, keepdims=True)
        if sinks is not None:
            den = den + jnp.exp(sinks[half] - m)
        outs.append(jnp.dot(p.astype(BF16), v2, preferred_element_type=F32) / den)
        lses.append(m + jnp.log(den))
    return jnp.where(low, outs[0], outs[1]), jnp.where(low, lses[0], lses[1])


def _swa_attn_kernel(sink_ref, q_ref, kc_ref, kp_ref, vc_ref, vp_ref, o_ref, kf, vf, *, pairs_per_kv, tq):
    n = pl.program_id(1)
    kf[0:ATTN_BLOCK, :] = kp_ref[...]
    kf[ATTN_BLOCK:, :] = kc_ref[...]
    vf[0:ATTN_BLOCK, :] = vp_ref[...]
    vf[ATTN_BLOCK:, :] = vc_ref[...]
    bias_any, bias_first = _band_biases(SWA_WINDOW - 1)

    def sub_block(j, carry):
        r0 = pl.multiple_of(j * ATTN_BLOCK, ATTN_BLOCK)
        bias = jnp.where(jnp.logical_and(n == 0, j == 0), bias_first, bias_any)
        for hp in range(SWA_Q_HEADS // 2):
            cols = slice(hp * LANES, (hp + 1) * LANES)
            kv_cols = slice((hp // pairs_per_kv) * LANES, (hp // pairs_per_kv + 1) * LANES)
            o, _ = _attend_pair(q_ref[pl.ds(r0, ATTN_BLOCK), cols], kf[pl.ds(r0, 2 * ATTN_BLOCK), kv_cols],
                                vf[pl.ds(r0, 2 * ATTN_BLOCK), kv_cols], bias, (sink_ref[2 * hp], sink_ref[2 * hp + 1]))
            o_ref[pl.ds(r0, ATTN_BLOCK), cols] = o.astype(o_ref.dtype)
        return carry

    lax.fori_loop(0, tq // ATTN_BLOCK, sub_block, 0)


def _dil_attn_kernel(q_ref, kc_ref, kp_ref, vc_ref, vp_ref, o_ref, l_ref, *, dil, max_dist):
    n = pl.program_id(1)
    rows = q_ref.shape[0]
    sub_rows = rows // dil
    bias_any, bias_first = _band_biases(max_dist)
    bias_head = jnp.where(n == 0, bias_first, bias_any)

    def rows_of(ref, r, start, count):
        if dil == 1:
            return ref[start:start + count, :]
        return ref[pl.ds(start * dil + r, count, stride=dil), :]

    for r in range(dil):
        k_prev = rows_of(kp_ref, r, 0, ATTN_BLOCK).astype(BF16)
        v_prev = rows_of(vp_ref, r, 0, ATTN_BLOCK).astype(BF16)
        for j in range(sub_rows // ATTN_BLOCK):
            start = j * ATTN_BLOCK
            k_cur = rows_of(kc_ref, r, start, ATTN_BLOCK).astype(BF16)
            v_cur = rows_of(vc_ref, r, start, ATTN_BLOCK).astype(BF16)
            q2 = rows_of(q_ref, r, start, ATTN_BLOCK).astype(BF16)
            o, l = _attend_pair(q2, jnp.concatenate([k_prev, k_cur], axis=0), jnp.concatenate([v_prev, v_cur], axis=0),
                                bias_head if j == 0 else bias_any, None)
            if dil == 1:
                o_ref[start:start + ATTN_BLOCK, :] = o
                l_ref[start:start + ATTN_BLOCK, :] = l
            else:
                o_ref[pl.ds(start * dil + r, ATTN_BLOCK, stride=dil), :] = o
                l_ref[pl.ds(start * dil + r, ATTN_BLOCK, stride=dil), :] = l
            k_prev, v_prev = k_cur, v_cur


def _attn_tile(length):
    return min(512, length)


def _swa_attention(qkv, sinks, batch, seq):
    tq = _attn_tile(seq)
    qkv = qkv.reshape(batch, seq, A_Q + 2 * A_KV_DUP)
    k_blk = A_Q // A_KV_DUP
    v_blk = k_blk + 1
    sub = tq // ATTN_BLOCK

    def prev(n):
        return jnp.maximum(n * sub - 1, 0)

    kern = functools.partial(_swa_attn_kernel, pairs_per_kv=SWA_Q_HEADS // SWA_KV_HEADS // 2, tq=tq)
    return pl.pallas_call(
        kern,
        grid=(batch, seq // tq),
        in_specs=[
            pl.BlockSpec(memory_space=pltpu.SMEM),
            pl.BlockSpec((None, tq, A_Q), lambda b, n: (b, n, 0)),
            pl.BlockSpec((None, tq, A_KV_DUP), lambda b, n: (b, n, k_blk)),
            pl.BlockSpec((None, ATTN_BLOCK, A_KV_DUP), lambda b, n: (b, prev(n), k_blk)),
            pl.BlockSpec((None, tq, A_KV_DUP), lambda b, n: (b, n, v_blk)),
            pl.BlockSpec((None, ATTN_BLOCK, A_KV_DUP), lambda b, n: (b, prev(n), v_blk)),
        ],
        out_specs=pl.BlockSpec((None, tq, A_Q), lambda b, n: (b, n, 0)),
        out_shape=jax.ShapeDtypeStruct((batch, seq, A_Q), BF16),
        scratch_shapes=[pltpu.VMEM((tq + ATTN_BLOCK, A_KV_DUP), BF16), pltpu.VMEM((tq + ATTN_BLOCK, A_KV_DUP), BF16)],
        compiler_params=_cparams("arbitrary", "arbitrary"),
        name="swa_attn",
    )(sinks, qkv, qkv, qkv, qkv, qkv).reshape(batch * seq, A_Q)


def _dilated_attention(qk, v, batch, seq, gi):
    window, dil = DIL_CONFIGS[gi]
    back = ATTN_BLOCK * dil
    rows = max(_attn_tile(seq), back)
    assert seq % rows == 0 and rows % back == 0
    pairs = DIL_HEADS_PER_GROUP // 2
    q_blk, k_blk, v_blk = gi * pairs, (C_QKV // LANES) + gi * pairs, gi * pairs
    per_seq = seq // rows
    ratio = rows // back

    def cur(col0):
        return pl.BlockSpec((rows, LANES), lambda b, n, hp: (b * per_seq + n, col0 + hp))

    def prev(col0):
        return pl.BlockSpec((back, LANES), lambda b, n, hp: (b * per_seq * ratio + jnp.maximum(n * ratio - 1, 0), col0 + hp))

    out_sds = jax.ShapeDtypeStruct((batch * seq, C_OUT), F32)
    out_spec = pl.BlockSpec((rows, LANES), lambda b, n, hp: (b * per_seq + n, hp))
    return pl.pallas_call(
        functools.partial(_dil_attn_kernel, dil=dil, max_dist=window // dil),
        grid=(batch, per_seq, pairs),
        in_specs=[cur(q_blk), cur(k_blk), prev(k_blk), cur(v_blk), prev(v_blk)],
        out_specs=[out_spec, out_spec],
        out_shape=[out_sds, out_sds],
        compiler_params=_cparams("arbitrary", "arbitrary", "arbitrary"),
        name=f"dil_attn_{dil}",
    )(qk, qk, qk, v, v)


def _ssd_kernel(z_ref, xs_ref, bc_ref, dt_ref, cwx_ref, cwbc_ref, cbx_ref, cbbc_ref, dtb_ref, alog_ref,
                dsk_ref, nw_ref, o_ref, extx, extbc, hst):
    q = SSD_CHUNK
    c = pl.program_id(1)

    @pl.when(c == 0)
    def _():
        extx[0:SUBLANES, :] = jnp.zeros((SUBLANES, SSD_INNER), F32)
        extbc[0:SUBLANES, :] = jnp.zeros((SUBLANES, SSD_BC_WIDTH), F32)
        hst[...] = jnp.zeros(hst.shape, F32)

    extx[SUBLANES:, :] = xs_ref[...]
    extbc[SUBLANES:, :] = bc_ref[...]

    def conv_silu(ext, cw_ref, cb_ref):
        acc = cb_ref[...]
        for k in range(SSD_CONV):
            start = SUBLANES - (SSD_CONV - 1) + k
            acc = acc + cw_ref[k:k + 1, :] * ext[start:start + q, :]
        return acc * _sigmoid(acc)

    xa = conv_silu(extx, cwx_ref, cbx_ref)
    bca = conv_silu(extbc, cwbc_ref, cbbc_ref)
    extx[0:SUBLANES, :] = xs_ref[q - SUBLANES:q, :]
    extbc[0:SUBLANES, :] = bc_ref[q - SUBLANES:q, :]

    dtv = dt_ref[...] + dtb_ref[...]
    dt = jnp.maximum(dtv, 0.0) + jnp.log(1.0 + jnp.exp(-jnp.abs(dtv)))
    a = -jnp.exp(alog_ref[...])
    rr = lax.broadcasted_iota(I32, (q, q), 0)
    cc = lax.broadcasted_iota(I32, (q, q), 1)
    causal = rr >= cc
    tril = jnp.where(causal, 1.0, 0.0).astype(F32)
    acs = jnp.dot(tril, dt * a, preferred_element_type=F32, precision=lax.Precision.HIGHEST)
    acs_t = acs.T

    low = lax.broadcasted_iota(I32, (q, LANES), 1) < SSD_HEADDIM
    low_state = lax.broadcasted_iota(I32, (SSD_STATE, LANES), 1) < SSD_HEADDIM
    pairs_per_group = SSD_HEADS_PER_GROUP // 2

    def per_head(h0, fn):
        return jnp.where(low, fn(h0), fn(h0 + 1))

    for g in range(SSD_GROUPS):
        bg = bca[:, g * SSD_STATE:(g + 1) * SSD_STATE]
        cg = bca[:, (SSD_GROUPS + g) * SSD_STATE:(SSD_GROUPS + g + 1) * SSD_STATE].astype(BF16)
        bt = bg.T.astype(BF16)
        cb = jnp.dot(cg, bt, preferred_element_type=F32)
        ys = []
        for pi in range(pairs_per_group):
            pair = g * pairs_per_group + pi
            h0 = 2 * pair
            cols = slice(pair * LANES, (pair + 1) * LANES)
            xp = xa[:, cols]
            xdt = xp * per_head(h0, lambda h: dt[:, h:h + 1])
            m_both = jnp.concatenate(
                [(cb * jnp.exp(jnp.where(causal, acs[:, h:h + 1] - acs_t[h:h + 1, :], NEG_INF))).astype(BF16)
                 for h in (h0, h0 + 1)], axis=1)
            x_both = jnp.concatenate([jnp.where(low, xdt, 0.0), jnp.where(low, 0.0, xdt)], axis=0).astype(BF16)
            y = jnp.dot(m_both, x_both, preferred_element_type=F32)
            hprev = hst[pair]
            y = y + jnp.dot(cg, hprev.astype(BF16), preferred_element_type=F32) * per_head(h0, lambda h: jnp.exp(acs[:, h:h + 1]))
            y = y + xp * dsk_ref[:, cols]
            to_end = per_head(h0, lambda h: jnp.exp(acs[q - 1:q, h:h + 1] - acs[:, h:h + 1]))
            chunk_decay = jnp.where(low_state, jnp.exp(acs[q - 1:q, h0:h0 + 1]), jnp.exp(acs[q - 1:q, h0 + 1:h0 + 2]))
            hst[pair] = hprev * chunk_decay + jnp.dot(bt, (xdt * to_end).astype(BF16), preferred_element_type=F32)
            ys.append(y)
        lo, hi = g * SSD_GROUP_WIDTH, (g + 1) * SSD_GROUP_WIDTH
        zg = z_ref[:, lo:hi]
        yg = jnp.concatenate(ys, axis=1) * (zg * _sigmoid(zg))
        yg = yg * lax.rsqrt(jnp.mean(yg * yg, axis=-1, keepdims=True) + LN_EPS)
        o_ref[:, lo:hi] = (yg * nw_ref[:, lo:hi]).astype(o_ref.dtype)


def _ssd_mixer(misc, conv_w, conv_b, dt_bias, a_log, d_skip, norm_w, batch, seq):
    q = SSD_CHUNK
    nc = seq // q
    pad = LANES - SSD_HEADS
    row = lambda v: v.reshape(1, -1)
    args = [
        misc, misc, misc, misc,
        conv_w[:, :SSD_INNER], conv_w[:, SSD_INNER:], row(conv_b[:SSD_INNER]), row(conv_b[SSD_INNER:]),
        row(jnp.pad(dt_bias, (0, pad))), row(jnp.pad(a_log, (0, pad))),
        row(jnp.repeat(d_skip, SSD_HEADDIM)), row(norm_w),
    ]
    tok = lambda b, c: b * nc + c
    full = lambda shape: pl.BlockSpec(shape, lambda b, c: (0, 0))
    bc_blk = 2 * SSD_INNER // SSD_BC_WIDTH
    dt_blk = (2 * SSD_INNER + SSD_BC_WIDTH) // LANES
    return pl.pallas_call(
        _ssd_kernel,
        grid=(batch, nc),
        in_specs=[
            pl.BlockSpec((q, SSD_INNER), lambda b, c: (tok(b, c), 0)),
            pl.BlockSpec((q, SSD_INNER), lambda b, c: (tok(b, c), 1)),
            pl.BlockSpec((q, SSD_BC_WIDTH), lambda b, c: (tok(b, c), bc_blk)),
            pl.BlockSpec((q, LANES), lambda b, c: (tok(b, c), dt_blk)),
            full((SSD_CONV, SSD_INNER)), full((SSD_CONV, SSD_BC_WIDTH)), full((1, SSD_INNER)), full((1, SSD_BC_WIDTH)),
            full((1, LANES)), full((1, LANES)), full((1, SSD_INNER)), full((1, SSD_INNER)),
        ],
        out_specs=pl.BlockSpec((q, SSD_INNER), lambda b, c: (tok(b, c), 0)),
        out_shape=jax.ShapeDtypeStruct((batch * seq, SSD_INNER), BF16),
        scratch_shapes=[
            pltpu.VMEM((SUBLANES + q, SSD_INNER), F32),
            pltpu.VMEM((SUBLANES + q, SSD_BC_WIDTH), F32),
            pltpu.VMEM((SSD_HEADS // 2, SSD_STATE, LANES), F32),
        ],
        compiler_params=_cparams("arbitrary", "arbitrary"),
        name="ssd_mixer",
    )(*args)


def _merge_kernel(ysw_ref, yss_ref, o0_ref, l0_ref, o1_ref, l1_ref, o2_ref, l2_ref, g0_ref, g1_ref, g2_ref,
                  p0_ref, p1_ref, p2_ref, out_ref):
    l0, l1, l2 = l0_ref[...], l1_ref[...], l2_ref[...]
    m = jnp.maximum(jnp.maximum(l0, l1), l2)
    e0, e1, e2 = jnp.exp(l0 - m), jnp.exp(l1 - m), jnp.exp(l2 - m)
    inv = 1.0 / (e0 + e1 + e2)
    ydil = (o0_ref[...] * (e0 * inv) + o1_ref[...] * (e1 * inv) + o2_ref[...] * (e2 * inv)).astype(BF16)
    merged = g0_ref[...] * jnp.dot(ysw_ref[...], p0_ref[...], preferred_element_type=F32)
    merged = merged + g1_ref[...] * jnp.dot(yss_ref[...], p1_ref[...], preferred_element_type=F32)
    merged = merged + g2_ref[...] * jnp.dot(ydil, p2_ref[...], preferred_element_type=F32)
    out_ref[...] = merged.astype(out_ref.dtype)


def _merge(y_swa, y_ssd, dil, gates, p_swa, p_ssd, p_dil, tm, tn):
    t = y_swa.shape[0]
    d = p_swa.shape[1]
    nj = d // tn
    rows = lambda width: pl.BlockSpec((tm, width), lambda i, j: (i, 0))
    gate = lambda k: pl.BlockSpec((tm, tn), lambda i, j: (i, k * nj + j))
    wcol = lambda width: pl.BlockSpec((width, tn), lambda i, j: (0, j))
    dil_args = [a for pair in dil for a in pair]
    return pl.pallas_call(
        _merge_kernel,
        grid=(t // tm, nj),
        in_specs=[rows(A_Q), rows(SSD_INNER)] + [rows(C_OUT)] * 6 + [gate(0), gate(1), gate(2)]
        + [wcol(A_Q), wcol(SSD_INNER), wcol(C_OUT)],
        out_specs=pl.BlockSpec((tm, tn), lambda i, j: (i, j)),
        out_shape=jax.ShapeDtypeStruct((t, d), BF16),
        compiler_params=_cparams("arbitrary", "arbitrary"),
        name="merge",
    )(y_swa, y_ssd, *dil_args, gates, gates, gates, p_swa, p_ssd, p_dil)


def _layer_norm(h, g, b):
    mu = jnp.mean(h, axis=-1, keepdims=True)
    hc = h - mu
    var = jnp.mean(hc * hc, axis=-1, keepdims=True)
    return hc * lax.rsqrt(var + LN_EPS) * g + b


LN1_ROW_SPLIT = 2


def _ln1_router_kernel(x_ref, m_ref, w_ref, g_ref, b_ref, wr_ref, br_ref, xo_ref, ids_ref, comb_ref, cnt_ref, *, alpha):
    i = pl.program_id(0)
    tm = x_ref.shape[0]
    rows = tm // LN1_ROW_SPLIT
    lane = lax.broadcasted_iota(I32, (rows, LANES), 1).astype(F32)
    big = float(2 * LANES)

    def first_argmax(vals):
        top = jnp.max(vals, axis=-1, keepdims=True)
        return top, jnp.min(jnp.where(vals == top, lane, big), axis=-1, keepdims=True)

    tops = []
    for part in range(LN1_ROW_SPLIT):
        sl = slice(part * rows, (part + 1) * rows)
        h = alpha * x_ref[sl, :] + jnp.dot(m_ref[sl, :], w_ref[...], preferred_element_type=F32)
        y = _layer_norm(h, g_ref[...], b_ref[...])
        xo_ref[sl, :] = y

        logits = jnp.dot(y.astype(BF16), wr_ref[...], preferred_element_type=F32) + br_ref[...]
        gl = jnp.where(lane < N_GROUPS, logits, NEG_INF)
        gmax, gsel = first_argmax(gl)
        p_group = 1.0 / jnp.sum(jnp.exp(gl - gmax), axis=-1, keepdims=True)
        lo = N_GROUPS + gsel * EXPERTS_PER_GROUP
        el = jnp.where((lane >= lo) & (lane < lo + EXPERTS_PER_GROUP), logits, NEG_INF)
        v1, i1 = first_argmax(el)
        v2, i2 = first_argmax(jnp.where(lane == i1, NEG_INF, el))
        e21 = jnp.exp(v2 - v1)
        inv = 1.0 / (1.0 + e21)
        comb_ref[sl, :] = jnp.where(lane == 0.0, inv * p_group, jnp.where(lane == 1.0, e21 * inv * p_group, 0.0))
        tops.append((i1, i2))

    @pl.when(i == 0)
    def _():
        cnt_ref[...] = jnp.zeros(cnt_ref.shape, F32)

    i1 = jnp.concatenate([t[0] for t in tops], axis=0)
    i2 = jnp.concatenate([t[1] for t in tops], axis=0)
    lane = lax.broadcasted_iota(I32, (tm, LANES), 1).astype(F32)
    pick1 = jnp.where(lane == i1, 1.0, 0.0)
    pick2 = jnp.where(lane == i2, 1.0, 0.0)
    picks = pick1 + pick2
    earlier = lax.broadcasted_iota(I32, (tm, tm), 0) > lax.broadcasted_iota(I32, (tm, tm), 1)
    before = jnp.dot(jnp.where(earlier, 1.0, 0.0).astype(BF16), picks.astype(BF16), preferred_element_type=F32) + cnt_ref[...]
    rank1 = jnp.sum(pick1 * before, axis=-1, keepdims=True)
    rank2 = jnp.sum(pick2 * before, axis=-1, keepdims=True)
    cnt_ref[...] = cnt_ref[...] + jnp.sum(picks, axis=0, keepdims=True)
    ids = jnp.where(lane == 0.0, i1 - N_GROUPS, jnp.where(lane == 1.0, i2 - N_GROUPS,
                    jnp.where(lane == 2.0, rank1, jnp.where(lane == 3.0, rank2, 0.0))))
    for c in range(tm // LANES):
        ids_ref[:, c * LANES:(c + 1) * LANES] = ids[c * LANES:(c + 1) * LANES, :].T[0:SUBLANES, :].astype(I32)


def _ln1_router(x, merged, w_out, g, b, wr, br, alpha, tm):
    t, d = x.shape
    rows = lambda width: pl.BlockSpec((tm, width), lambda i: (i, 0))
    full = lambda shape: pl.BlockSpec(shape, lambda i: (0, 0))
    once = lambda shape: pl.BlockSpec(shape, lambda i: (0, 0), pipeline_mode=pl.Buffered(1))
    return pl.pallas_call(
        functools.partial(_ln1_router_kernel, alpha=alpha),
        grid=(t // tm,),
        in_specs=[rows(d), rows(d), once((d, d)), full((1, d)), full((1, d)), once((d, LANES)), full((1, LANES))],
        out_specs=[rows(d), pl.BlockSpec((SUBLANES, tm), lambda i: (0, i)), rows(LANES), full((1, LANES))],
        out_shape=[jax.ShapeDtypeStruct((t, d), F32),
                   jax.ShapeDtypeStruct((SUBLANES, t), I32), jax.ShapeDtypeStruct((t, LANES), F32),
                   jax.ShapeDtypeStruct((1, LANES), F32)],
        compiler_params=_cparams("arbitrary"),
        name="ln1_router",
    )(x, merged, w_out, g.reshape(1, d), b.reshape(1, d), wr, br)


MOE_GATHER_AHEAD = 3
MOE_X_BUFFERS = 4


def _moe_kernel(be_ref, dest_ref, pad_lo_ref, pad_hi_ref, nu_ref, x_hbm, w1_ref, w3_ref, w2_ref, y_hbm,
                xbuf0, xbuf1, xbuf2, xbuf3, ybuf0, ybuf1, w1b, w3b, w2b, tok_ref, dst_ref, gsem, ssem, *, n_tok):
    b = pl.program_id(0)
    n_used = nu_ref[0]
    xbufs, ybufs = (xbuf0, xbuf1, xbuf2, xbuf3), (ybuf0, ybuf1)
    trash_row0 = TOP_K * n_tok

    def entry(blk, r):
        return (blk + 1) * MOE_BLOCK + r

    def gather_copy(blk, p, r):
        return pltpu.make_async_copy(
            x_hbm.at[pl.ds(tok_ref[entry(blk, r)], 1), :], xbufs[p].at[pl.ds(r, 1), :], gsem.at[p])

    def scatter_copy(blk, p, r):
        return pltpu.make_async_copy(
            ybufs[p].at[pl.ds(r, 1), :], y_hbm.at[pl.ds(dst_ref[entry(blk, r)], 1), :], ssem.at[p])

    def for_rows(fn):
        def chunk(c, carry):
            for u in range(MOE_DMA_UNROLL):
                fn(c * MOE_DMA_UNROLL + u)
            return carry
        lax.fori_loop(0, MOE_BLOCK // MOE_DMA_UNROLL, chunk, 0)

    @pl.when(b == 0)
    def _():
        def pad_entries(lo, hi):
            def pad_entry(s, carry):
                tok_ref[s] = 0
                dst_ref[s] = trash_row0 + lax.bitwise_and(s, MOE_BLOCK - 1)
                return carry
            lax.fori_loop(lo, hi, pad_entry, 0)

        pad_entries(0, MOE_BLOCK)
        lax.fori_loop(0, N_EXPERTS, lambda e, c: (pad_entries(entry(0, pad_lo_ref[e]), entry(0, pad_hi_ref[e])), c)[1], 0)
        pad_entries(entry(n_used, 0), entry(n_used + MOE_GATHER_AHEAD, 0))

        for k in range(TOP_K):
            def fill(c, carry):
                for u in range(MOE_DMA_UNROLL):
                    tkn = c * MOE_DMA_UNROLL + u
                    s = dest_ref[k * n_tok + tkn] + MOE_BLOCK
                    tok_ref[s] = tkn
                    dst_ref[s] = k * n_tok + tkn
                return carry
            lax.fori_loop(0, n_tok // MOE_DMA_UNROLL, fill, 0)

        ybuf1[...] = jnp.zeros(ybuf1.shape, F32)
        for ahead in range(MOE_GATHER_AHEAD):
            for_rows(lambda r: gather_copy(ahead, ahead, r).start())

    def step(p):
        py = p % 2
        qy = 1 - py
        px_next = (p + MOE_GATHER_AHEAD) % MOE_X_BUFFERS
        for_rows(lambda r: gather_copy(b, p, r).wait())

        @pl.when(b >= 1)
        def _():
            for_rows(lambda r: scatter_copy(b - 2, py, r).wait())

        @pl.when(jnp.logical_or(b == 0, be_ref[b] != be_ref[jnp.maximum(b - 1, 0)]))
        def _():
            w1b[...] = w1_ref[...].astype(BF16)
            w3b[...] = w3_ref[...].astype(BF16)
            w2b[...] = w2_ref[...].astype(BF16)

        def issue(part):
            def row(r, carry):
                gather_copy(b + MOE_GATHER_AHEAD, px_next, r).start()
                scatter_copy(b - 1, qy, r).start()
                return carry
            lax.fori_loop(part * MOE_BLOCK // 4, (part + 1) * MOE_BLOCK // 4, row, 0, unroll=True)

        xb = xbufs[p][...].astype(BF16)
        issue(0)
        h1 = jnp.dot(xb, w1b[...], preferred_element_type=F32)
        issue(1)
        h3 = jnp.dot(xb, w3b[...], preferred_element_type=F32)
        issue(2)
        h = (h1 * _sigmoid(h1) * h3).astype(BF16)
        issue(3)
        ybufs[py][...] = jnp.dot(h, w2b[...], preferred_element_type=F32)

        @pl.when(b == n_used - 1)
        def _():
            for_rows(lambda r: scatter_copy(b - 1, qy, r).wait())
            for_rows(lambda r: scatter_copy(b, py, r).start())
            for_rows(lambda r: scatter_copy(b, py, r).wait())
            for ahead in range(1, MOE_GATHER_AHEAD + 1):
                for_rows(lambda r: gather_copy(b + ahead, (p + ahead) % MOE_X_BUFFERS, r).wait())

    for p in range(MOE_X_BUFFERS):
        pl.when(jnp.logical_and(b < n_used, b % MOE_X_BUFFERS == p))(functools.partial(step, p))


def _moe_experts(x1, plan, w1, w3, w2, layer):
    t, d = x1.shape
    block_e, dest, pad_lo, pad_hi, n_used = plan
    n_blocks = block_e.shape[0]
    table = pltpu.SMEM(((n_blocks + 1 + MOE_GATHER_AHEAD) * MOE_BLOCK,), I32)
    row_buf = pltpu.VMEM((MOE_BLOCK, d), F32)
    wspec = lambda shape: pl.BlockSpec((None, None) + shape, lambda b, be, *_: (layer, be[b], 0, 0))
    grid_spec = pltpu.PrefetchScalarGridSpec(
        num_scalar_prefetch=5,
        grid=(n_blocks,),
        in_specs=[pl.BlockSpec(memory_space=pl.ANY), wspec((d, D_EXPERT)), wspec((d, D_EXPERT)), wspec((D_EXPERT, d))],
        out_specs=pl.BlockSpec(memory_space=pl.ANY),
        scratch_shapes=[row_buf] * (MOE_X_BUFFERS + 2) + [
            pltpu.VMEM((d, D_EXPERT), BF16), pltpu.VMEM((d, D_EXPERT), BF16), pltpu.VMEM((D_EXPERT, d), BF16),
            table, table,
            pltpu.SemaphoreType.DMA((MOE_X_BUFFERS,)),
            pltpu.SemaphoreType.DMA((2,)),
        ],
    )
    return pl.pallas_call(
        functools.partial(_moe_kernel, n_tok=t),
        grid_spec=grid_spec,
        out_shape=jax.ShapeDtypeStruct((TOP_K * t + MOE_BLOCK, d), F32),
        compiler_params=_cparams("arbitrary"),
        name="moe_experts",
    )(block_e, dest, pad_lo, pad_hi, n_used, x1, w1, w3, w2)


def _ln2_kernel(x_ref, y0_ref, y1_ref, comb_ref, g_ref, b_ref, xo_ref, xb_ref, *, alpha):
    comb = comb_ref[...]
    h = alpha * x_ref[...] + comb[:, 0:1] * y0_ref[...] + comb[:, 1:2] * y1_ref[...]
    y = _layer_norm(h, g_ref[...], b_ref[...])
    xo_ref[...] = y
    xb_ref[...] = y.astype(BF16)


def _ln2(x1, y2, comb, g, b, alpha, tm):
    t, d = x1.shape
    nblk = t // tm
    rows = lambda width: pl.BlockSpec((tm, width), lambda i: (i, 0))
    full = lambda shape: pl.BlockSpec(shape, lambda i: (0, 0))
    return pl.pallas_call(
        functools.partial(_ln2_kernel, alpha=alpha),
        grid=(nblk,),
        in_specs=[rows(d), rows(d), pl.BlockSpec((tm, d), lambda i: (nblk + i, 0)), rows(LANES), full((1, d)), full((1, d))],
        out_specs=[rows(d), rows(d)],
        out_shape=[jax.ShapeDtypeStruct((t, d), F32), jax.ShapeDtypeStruct((t, d), BF16)],
        compiler_params=_cparams("arbitrary"),
        name="ln2",
    )(x1, y2, y2, comb, g.reshape(1, d), b.reshape(1, d))


def _dispatch_plan(ids, counts, n_tok):
    n_assign = n_tok * TOP_K
    n_blocks = -(-n_assign // MOE_BLOCK) + N_EXPERTS
    experts = jnp.arange(N_EXPERTS, dtype=I32)
    padded = (counts + MOE_BLOCK - 1) // MOE_BLOCK * MOE_BLOCK
    pad_end = jnp.cumsum(padded)
    pad_start = pad_end - padded
    slot0 = jnp.sum(jnp.where(ids[0:TOP_K, :, None] == experts, pad_start, 0), axis=-1)
    dest = (slot0 + ids[TOP_K:2 * TOP_K]).reshape(n_assign)
    starts = jnp.arange(n_blocks, dtype=I32) * MOE_BLOCK
    block_e = jnp.minimum(jnp.sum((pad_end[None, :] <= starts[:, None]).astype(I32), axis=1), N_EXPERTS - 1)
    return block_e, dest, pad_start + counts, pad_end, pad_end[-1:] // MOE_BLOCK


def _rope_tables(seq):
    inv_freq = 1.0 / (ROPE_THETA ** (jnp.arange(0, HEAD_DIM, 2, dtype=F32) / HEAD_DIM))
    ang = jnp.arange(seq, dtype=F32)[:, None] * inv_freq[None, :]
    cos, sin = jnp.cos(ang), jnp.sin(ang)
    reps = LANES // HEAD_DIM
    return jnp.tile(jnp.concatenate([cos, cos], axis=1), (1, reps)), jnp.tile(jnp.concatenate([-sin, sin], axis=1), (1, reps))


def _tile(total, want):
    return min(total, want)


def kernel(x, w_in, b_in, attn_sinks, conv_w, conv_b, dt_bias, a_log, d_skip, ssd_norm_w, proj_swa, proj_ssd, proj_dil, w_out, ln1_g, ln1_b, router_group_w, router_group_b, router_expert_w, router_expert_b, expert_w1, expert_w3, expert_w2, ln2_g, ln2_b):
    batch, seq, d = x.shape
    depth = w_in.shape[0]
    t = batch * seq
    alpha = (2 * depth) ** 0.25
    rope = _rope_tables(seq)
    tm = _tile(seq, 512)
    tmp = _tile(seq, 1024)

    sizes = (A_Q, A_KV, A_KV, SSD_INNER, SSD_INNER + SSD_BC_WIDTH, SSD_HEADS, C_QKV, C_QKV, C_QKV, 3 * d)
    offs = [0]
    for s in sizes:
        offs.append(offs[-1] + s)
    o_qa, o_ka, o_va, o_z, o_xbc, o_dt, o_qc, o_kc, o_vc, o_gate, o_end = offs

    n_a, n_misc, n_gate = o_z - o_qa, o_qc - o_z, o_end - o_gate
    dt_pad = LANES - SSD_HEADS

    def piece(lo, hi, pad=0):
        wp, bp = w_in[:, :, lo:hi], b_in[:, None, lo:hi]
        if pad:
            wp, bp = jnp.pad(wp, ((0, 0), (0, 0), (0, pad))), jnp.pad(bp, ((0, 0), (0, 0), (0, pad)))
        return wp.astype(BF16), bp

    w_a, b_a = piece(o_qa, o_z)
    w_misc, b_misc = piece(o_z, o_qc, dt_pad)
    w_c, b_c = piece(o_qc, o_gate)
    w_g, b_g = piece(o_gate, o_end)
    p_swa, p_ssd, p_dil, w_o = (m.astype(BF16) for m in (proj_swa, proj_ssd, proj_dil, w_out))
    wr_all = jnp.concatenate(
        [router_group_w, router_expert_w.transpose(0, 2, 1, 3).reshape(depth, d, N_EXPERTS)], axis=2)
    wr_all = jnp.pad(wr_all, ((0, 0), (0, 0), (0, LANES - N_GROUPS - N_EXPERTS))).astype(BF16)
    br_all = jnp.pad(jnp.concatenate([router_group_b, router_expert_b.reshape(depth, N_EXPERTS)], axis=1),
                     ((0, 0), (0, LANES - N_GROUPS - N_EXPERTS)))

    xf = x.reshape(t, d)
    xb = xf.astype(BF16)
    for l in range(depth):
        qkv_a = _proj(xb, w_a, b_a, l, 0, n_a, BF16, tmp, n_a, seq, rope=rope, rope_blocks=(A_Q + A_KV) // LANES,
                      dup_from=A_Q // LANES, name="proj_swa_qkv")
        qk_c = _proj(xb, w_c, b_c, l, 0, 2 * C_QKV, F32, tmp, C_QKV, seq, rope=rope, rope_blocks=C_QKV // LANES, name="proj_dil_qk")
        v_c = _proj(xb, w_c, b_c, l, 2 * C_QKV, C_QKV, F32, tmp, C_QKV, seq, name="proj_dil_v")
        misc = _proj(xb, w_misc, b_misc, l, 0, n_misc + dt_pad, F32, tmp, (n_misc + dt_pad) // 3, seq, name="proj_ssd")
        gates = _proj(xb, w_g, b_g, l, 0, n_gate, F32, tmp, 1024, seq, act="sigmoid", name="proj_gates")

        y_swa = _swa_attention(qkv_a, attn_sinks[l], batch, seq)
        y_ssd = _ssd_mixer(misc, conv_w[l], conv_b[l], dt_bias[l], a_log[l], d_skip[l], ssd_norm_w[l], batch, seq)
        dil = [_dilated_attention(qk_c, v_c, batch, seq, gi) for gi in range(len(DIL_CONFIGS))]

        merged = _merge(y_swa, y_ssd, dil, gates, p_swa[l], p_ssd[l], p_dil[l], tm, 1024)
        x1, ids, comb, cnt = _ln1_router(xf, merged, w_o[l], ln1_g[l], ln1_b[l], wr_all[l], br_all[l].reshape(1, LANES), alpha, tm)

        plan = _dispatch_plan(ids, cnt[0, N_GROUPS:N_GROUPS + N_EXPERTS].astype(I32), t)
        y2 = _moe_experts(x1, plan, expert_w1, expert_w3, expert_w2, l)
        xf, xb = _ln2(x1, y2, comb, ln2_g[l], ln2_b[l], alpha, tm)
    return xf.reshape(batch, seq, d)
```

```python
import functools

import jax
import jax.numpy as jnp
from jax import lax
from jax.experimental import pallas as pl
from jax.experimental.pallas import tpu as pltpu

F32 = jnp.float32
BF16 = jnp.bfloat16
I32 = jnp.int32

LANES = 128
SUBLANES = 8
VMEM_LIMIT_BYTES = 48 * 1024 * 1024

HEAD_DIM = 64
HALF_HEAD = HEAD_DIM // 2
ROPE_THETA = 10000.0
ATTN_BLOCK = 128
LN_EPS = 1e-5

SWA_Q_HEADS = 16
SWA_KV_HEADS = 2
SWA_WINDOW = 128

SSD_HEADS = 24
SSD_HEADDIM = 64
SSD_INNER = SSD_HEADS * SSD_HEADDIM
SSD_GROUPS = 4
SSD_STATE = 128
SSD_CONV = 4
SSD_CHUNK = 128
SSD_HEADS_PER_GROUP = SSD_HEADS // SSD_GROUPS
SSD_GROUP_WIDTH = SSD_INNER // SSD_GROUPS
SSD_BC_WIDTH = 2 * SSD_GROUPS * SSD_STATE

DIL_CONFIGS = ((128, 1), (512, 4), (2048, 16))
DIL_HEADS_PER_GROUP = 6
DIL_HEADS = DIL_HEADS_PER_GROUP * len(DIL_CONFIGS)

A_Q = SWA_Q_HEADS * HEAD_DIM
A_KV = SWA_KV_HEADS * HEAD_DIM
A_KV_DUP = 2 * A_KV
C_QKV = DIL_HEADS * HEAD_DIM
C_OUT = DIL_HEADS_PER_GROUP * HEAD_DIM

N_GROUPS = 4
EXPERTS_PER_GROUP = 8
N_EXPERTS = N_GROUPS * EXPERTS_PER_GROUP
TOP_K = 2
D_EXPERT = 512
MOE_BLOCK = 128
MOE_DMA_UNROLL = 8

NEG_INF = float("-inf")


def _cparams(*sem):
    return pltpu.CompilerParams(dimension_semantics=sem, vmem_limit_bytes=VMEM_LIMIT_BYTES)


def _sigmoid(v):
    return 1.0 / (1.0 + jnp.exp(-v))


def _proj_kernel(*refs, rope_blocks, dup_from, act):
    if rope_blocks:
        x_ref, w_ref, b_ref, cos_ref, sin_ref, o_ref = refs
    else:
        x_ref, w_ref, b_ref, o_ref = refs
    acc = jnp.dot(x_ref[...], w_ref[...], preferred_element_type=F32) + b_ref[...]
    n_blocks = acc.shape[1] // LANES
    if rope_blocks or dup_from < n_blocks:
        lane = lax.broadcasted_iota(I32, (acc.shape[0], LANES), 1)
        first_half = (lane % HEAD_DIM) < HALF_HEAD
        low_head = lane < HEAD_DIM
        out_c = 0
        for c in range(n_blocks):
            v = acc[:, c * LANES:(c + 1) * LANES]
            if c < rope_blocks:
                partner = jnp.where(first_half, pltpu.roll(v, LANES - HALF_HEAD, 1), pltpu.roll(v, HALF_HEAD, 1))
                v = v * cos_ref[...] + partner * sin_ref[...]
            if c >= dup_from:
                other = pltpu.roll(v, HEAD_DIM, 1)
                outs = (jnp.where(low_head, v, other), jnp.where(low_head, other, v))
            else:
                outs = (v,)
            for o in outs:
                o_ref[:, out_c * LANES:(out_c + 1) * LANES] = o.astype(o_ref.dtype)
                out_c += 1
    elif act == "sigmoid":
        o_ref[...] = _sigmoid(acc).astype(o_ref.dtype)
    else:
        o_ref[...] = acc.astype(o_ref.dtype)


def _proj(xb, w, b, layer, col0, n, out_dtype, tm, tn, seq, rope=None, rope_blocks=0, dup_from=None, act=None, name="proj"):
    t, d = xb.shape
    assert t % tm == 0 and n % tn == 0 and seq % tm == 0 and col0 % tn == 0
    j0 = col0 // tn
    in_blocks = tn // LANES
    if dup_from is None:
        dup_from = in_blocks
    assert dup_from == in_blocks or n == tn
    tn_out = tn + (in_blocks - dup_from) * LANES
    in_specs = [
        pl.BlockSpec((tm, d), lambda j, i: (i, 0)),
        pl.BlockSpec((None, d, tn), lambda j, i: (layer, 0, j0 + j)),
        pl.BlockSpec((None, 1, tn), lambda j, i: (layer, 0, j0 + j)),
    ]
    args = [xb, w, b]
    if rope_blocks:
        sblocks = seq // tm
        in_specs += [pl.BlockSpec((tm, LANES), lambda j, i: (i % sblocks, 0))] * 2
        args += list(rope)
    return pl.pallas_call(
        functools.partial(_proj_kernel, rope_blocks=rope_blocks, dup_from=dup_from, act=act),
        grid=(n // tn, t // tm),
        in_specs=in_specs,
        out_specs=pl.BlockSpec((tm, tn_out), lambda j, i: (i, j)),
        out_shape=jax.ShapeDtypeStruct((t, n // tn * tn_out), out_dtype),
        compiler_params=_cparams("arbitrary", "arbitrary"),
        name=name,
    )(*args)


def _band_biases(max_dist):
    row = lax.broadcasted_iota(I32, (ATTN_BLOCK, 2 * ATTN_BLOCK), 0)
    col = lax.broadcasted_iota(I32, (ATTN_BLOCK, 2 * ATTN_BLOCK), 1)
    dist = row + ATTN_BLOCK - col
    in_band = (dist >= 0) & (dist <= max_dist)
    bias_any = jnp.where(in_band, 0.0, NEG_INF).astype(F32)
    bias_first = jnp.where(in_band & (col >= ATTN_BLOCK), 0.0, NEG_INF).astype(F32)
    return bias_any, bias_first


def _attend_pair(q2, k2, v2, bias, sinks):
    lane = lax.broadcasted_iota(I32, (ATTN_BLOCK, LANES), 1)
    low = lane < HEAD_DIM
    scale = HEAD_DIM ** -0.5
    outs, lses = [], []
    for half in range(2):
        qh = jnp.where(low if half == 0 else jnp.logical_not(low), q2, jnp.zeros_like(q2))
        s = lax.dot_general(qh, k2, (((1,), (1,)), ((), ())), preferred_element_type=F32) * scale
        s = s + bias
        m = jnp.max(s, axis=-1, keepdims=True)
        if sinks is not None:
            m = jnp.maximum(m, sinks[half])
        p = jnp.exp(s - m)
        den = jnp.sum(p, axis=-1, keepdims=True)
        if sinks is not None:
            den = den + jnp.exp(sinks[half] - m)
        outs.append(jnp.dot(p.astype(BF16), v2, preferred_element_type=F32) / den)
        lses.append(m + jnp.log(den))
    return jnp.where(low, outs[0], outs[1]), jnp.where(low, lses[0], lses[1])


def _swa_attn_kernel(sink_ref, q_ref, kc_ref, kp_ref, vc_ref, vp_ref, o_ref, kf, vf, *, pairs_per_kv, tq):
    n = pl.program_id(1)
    kf[0:ATTN_BLOCK, :] = kp_ref[...]
    kf[ATTN_BLOCK:, :] = kc_ref[...]
    vf[0:ATTN_BLOCK, :] = vp_ref[...]
    vf[ATTN_BLOCK:, :] = vc_ref[...]
    bias_any, bias_first = _band_biases(SWA_WINDOW - 1)

    def sub_block(j, carry):
        r0 = pl.multiple_of(j * ATTN_BLOCK, ATTN_BLOCK)
        bias = jnp.where(jnp.logical_and(n == 0, j == 0), bias_first, bias_any)
        for hp in range(SWA_Q_HEADS // 2):
            cols = slice(hp * LANES, (hp + 1) * LANES)
            kv_cols = slice((hp // pairs_per_kv) * LANES, (hp // pairs_per_kv + 1) * LANES)
            o, _ = _attend_pair(q_ref[pl.ds(r0, ATTN_BLOCK), cols], kf[pl.ds(r0, 2 * ATTN_BLOCK), kv_cols],
                                vf[pl.ds(r0, 2 * ATTN_BLOCK), kv_cols], bias, (sink_ref[2 * hp], sink_ref[2 * hp + 1]))
            o_ref[pl.ds(r0, ATTN_BLOCK), cols] = o.astype(o_ref.dtype)
        return carry

    lax.fori_loop(0, tq // ATTN_BLOCK, sub_block, 0)


def _dil_attn_kernel(q_ref, kc_ref, kp_ref, vc_ref, vp_ref, o_ref, l_ref, *, dil, max_dist):
    n = pl.program_id(1)
    rows = q_ref.shape[0]
    sub_rows = rows // dil
    bias_any, bias_first = _band_biases(max_dist)
    bias_head = jnp.where(n == 0, bias_first, bias_any)

    def rows_of(ref, r, start, count):
        if dil == 1:
            return ref[start:start + count, :]
        return ref[pl.ds(start * dil + r, count, stride=dil), :]

    for r in range(dil):
        k_prev = rows_of(kp_ref, r, 0, ATTN_BLOCK).astype(BF16)
        v_prev = rows_of(vp_ref, r, 0, ATTN_BLOCK).astype(BF16)
        for j in range(sub_rows // ATTN_BLOCK):
            start = j * ATTN_BLOCK
            k_cur = rows_of(kc_ref, r, start, ATTN_BLOCK).astype(BF16)
            v_cur = rows_of(vc_ref, r, start, ATTN_BLOCK).astype(BF16)
            q2 = rows_of(q_ref, r, start, ATTN_BLOCK).astype(BF16)
            o, l = _attend_pair(q2, jnp.concatenate([k_prev, k_cur], axis=0), jnp.concatenate([v_prev, v_cur], axis=0),
                                bias_head if j == 0 else bias_any, None)
            if dil == 1:
                o_ref[start:start + ATTN_BLOCK, :] = o
                l_ref[start:start + ATTN_BLOCK, :] = l
            else:
                o_ref[pl.ds(start * dil + r, ATTN_BLOCK, stride=dil), :] = o
                l_ref[pl.ds(start * dil + r, ATTN_BLOCK, stride=dil), :] = l
            k_prev, v_prev = k_cur, v_cur


def _attn_tile(length):
    return min(512, length)


def _swa_attention(qkv, sinks, batch, seq):
    tq = _attn_tile(seq)
    qkv = qkv.reshape(batch, seq, A_Q + 2 * A_KV_DUP)
    k_blk = A_Q // A_KV_DUP
    v_blk = k_blk + 1
    sub = tq // ATTN_BLOCK

    def prev(n):
        return jnp.maximum(n * sub - 1, 0)

    kern = functools.partial(_swa_attn_kernel, pairs_per_kv=SWA_Q_HEADS // SWA_KV_HEADS // 2, tq=tq)
    return pl.pallas_call(
        kern,
        grid=(batch, seq // tq),
        in_specs=[
            pl.BlockSpec(memory_space=pltpu.SMEM),
            pl.BlockSpec((None, tq, A_Q), lambda b, n: (b, n, 0)),
            pl.BlockSpec((None, tq, A_KV_DUP), lambda b, n: (b, n, k_blk)),
            pl.BlockSpec((None, ATTN_BLOCK, A_KV_DUP), lambda b, n: (b, prev(n), k_blk)),
            pl.BlockSpec((None, tq, A_KV_DUP), lambda b, n: (b, n, v_blk)),
            pl.BlockSpec((None, ATTN_BLOCK, A_KV_DUP), lambda b, n: (b, prev(n), v_blk)),
        ],
        out_specs=pl.BlockSpec((None, tq, A_Q), lambda b, n: (b, n, 0)),
        out_shape=jax.ShapeDtypeStruct((batch, seq, A_Q), BF16),
        scratch_shapes=[pltpu.VMEM((tq + ATTN_BLOCK, A_KV_DUP), BF16), pltpu.VMEM((tq + ATTN_BLOCK, A_KV_DUP), BF16)],
        compiler_params=_cparams("arbitrary", "arbitrary"),
        name="swa_attn",
    )(sinks, qkv, qkv, qkv, qkv, qkv).reshape(batch * seq, A_Q)


def _dilated_attention(qk, v, batch, seq, gi):
    window, dil = DIL_CONFIGS[gi]
    back = ATTN_BLOCK * dil
    rows = max(_attn_tile(seq), back)
    assert seq % rows == 0 and rows % back == 0
    pairs = DIL_HEADS_PER_GROUP // 2
    q_blk, k_blk, v_blk = gi * pairs, (C_QKV // LANES) + gi * pairs, gi * pairs
    per_seq = seq // rows
    ratio = rows // back

    def cur(col0):
        return pl.BlockSpec((rows, LANES), lambda b, n, hp: (b * per_seq + n, col0 + hp))

    def prev(col0):
        return pl.BlockSpec((back, LANES), lambda b, n, hp: (b * per_seq * ratio + jnp.maximum(n * ratio - 1, 0), col0 + hp))

    out_sds = jax.ShapeDtypeStruct((batch * seq, C_OUT), F32)
    out_spec = pl.BlockSpec((rows, LANES), lambda b, n, hp: (b * per_seq + n, hp))
    return pl.pallas_call(
        functools.partial(_dil_attn_kernel, dil=dil, max_dist=window // dil),
        grid=(batch, per_seq, pairs),
        in_specs=[cur(q_blk), cur(k_blk), prev(k_blk), cur(v_blk), prev(v_blk)],
        out_specs=[out_spec, out_spec],
        out_shape=[out_sds, out_sds],
        compiler_params=_cparams("arbitrary", "arbitrary", "arbitrary"),
        name=f"dil_attn_{dil}",
    )(qk, qk, qk, v, v)


def _ssd_kernel(z_ref, xs_ref, bc_ref, dt_ref, cwx_ref, cwbc_ref, cbx_ref, cbbc_ref, dtb_ref, alog_ref,
                dsk_ref, nw_ref, o_ref, extx, extbc, hst):
    q = SSD_CHUNK
    c = pl.program_id(1)

    @pl.when(c == 0)
    def _():
        extx[0:SUBLANES, :] = jnp.zeros((SUBLANES, SSD_INNER), F32)
        extbc[0:SUBLANES, :] = jnp.zeros((SUBLANES, SSD_BC_WIDTH), F32)
        hst[...] = jnp.zeros(hst.shape, F32)

    extx[SUBLANES:, :] = xs_ref[...]
    extbc[SUBLANES:, :] = bc_ref[...]

    def conv_silu(ext, cw_ref, cb_ref):
        acc = cb_ref[...]
        for k in range(SSD_CONV):
            start = SUBLANES - (SSD_CONV - 1) + k
            acc = acc + cw_ref[k:k + 1, :] * ext[start:start + q, :]
        return acc * _sigmoid(acc)

    xa = conv_silu(extx, cwx_ref, cbx_ref)
    bca = conv_silu(extbc, cwbc_ref, cbbc_ref)
    extx[0:SUBLANES, :] = xs_ref[q - SUBLANES:q, :]
    extbc[0:SUBLANES, :] = bc_ref[q - SUBLANES:q, :]

    dtv = dt_ref[...] + dtb_ref[...]
    dt = jnp.maximum(dtv, 0.0) + jnp.log(1.0 + jnp.exp(-jnp.abs(dtv)))
    a = -jnp.exp(alog_ref[...])
    rr = lax.broadcasted_iota(I32, (q, q), 0)
    cc = lax.broadcasted_iota(I32, (q, q), 1)
    causal = rr >= cc
    tril = jnp.where(causal, 1.0, 0.0).astype(F32)
    acs = jnp.dot(tril, dt * a, preferred_element_type=F32, precision=lax.Precision.HIGHEST)
    acs_t = acs.T

    low = lax.broadcasted_iota(I32, (q, LANES), 1) < SSD_HEADDIM
    low_state = lax.broadcasted_iota(I32, (SSD_STATE, LANES), 1) < SSD_HEADDIM
    pairs_per_group = SSD_HEADS_PER_GROUP // 2

    def per_head(h0, fn):
        return jnp.where(low, fn(h0), fn(h0 + 1))

    for g in range(SSD_GROUPS):
        bg = bca[:, g * SSD_STATE:(g + 1) * SSD_STATE]
        cg = bca[:, (SSD_GROUPS + g) * SSD_STATE:(SSD_GROUPS + g + 1) * SSD_STATE].astype(BF16)
        bt = bg.T.astype(BF16)
        cb = jnp.dot(cg, bt, preferred_element_type=F32)
        ys = []
        for pi in range(pairs_per_group):
            pair = g * pairs_per_group + pi
            h0 = 2 * pair
            cols = slice(pair * LANES, (pair + 1) * LANES)
            xp = xa[:, cols]
            xdt = xp * per_head(h0, lambda h: dt[:, h:h + 1])
            m_both = jnp.concatenate(
                [(cb * jnp.exp(jnp.where(causal, acs[:, h:h + 1] - acs_t[h:h + 1, :], NEG_INF))).astype(BF16)
                 for h in (h0, h0 + 1)], axis=1)
            x_both = jnp.concatenate([jnp.where(low, xdt, 0.0), jnp.where(low, 0.0, xdt)], axis=0).astype(BF16)
            y = jnp.dot(m_both, x_both, preferred_element_type=F32)
            hprev = hst[pair]
            y = y + jnp.dot(cg, hprev.astype(BF16), preferred_element_type=F32) * per_head(h0, lambda h: jnp.exp(acs[:, h:h + 1]))
            y = y + xp * dsk_ref[:, cols]
            to_end = per_head(h0, lambda h: jnp.exp(acs[q - 1:q, h:h + 1] - acs[:, h:h + 1]))
            chunk_decay = jnp.where(low_state, jnp.exp(acs[q - 1:q, h0:h0 + 1]), jnp.exp(acs[q - 1:q, h0 + 1:h0 + 2]))
            hst[pair] = hprev * chunk_decay + jnp.dot(bt, (xdt * to_end).astype(BF16), preferred_element_type=F32)
            ys.append(y)
        lo, hi = g * SSD_GROUP_WIDTH, (g + 1) * SSD_GROUP_WIDTH
        zg = z_ref[:, lo:hi]
        yg = jnp.concatenate(ys, axis=1) * (zg * _sigmoid(zg))
        yg = yg * lax.rsqrt(jnp.mean(yg * yg, axis=-1, keepdims=True) + LN_EPS)
        o_ref[:, lo:hi] = (yg * nw_ref[:, lo:hi]).astype(o_ref.dtype)


def _ssd_mixer(misc, conv_w, conv_b, dt_bias, a_log, d_skip, norm_w, batch, seq):
    q = SSD_CHUNK
    nc = seq // q
    pad = LANES - SSD_HEADS
    row = lambda v: v.reshape(1, -1)
    args = [
        misc, misc, misc, misc,
        conv_w[:, :SSD_INNER], conv_w[:, SSD_INNER:], row(conv_b[:SSD_INNER]), row(conv_b[SSD_INNER:]),
        row(jnp.pad(dt_bias, (0, pad))), row(jnp.pad(a_log, (0, pad))),
        row(jnp.repeat(d_skip, SSD_HEADDIM)), row(norm_w),
    ]
    tok = lambda b, c: b * nc + c
    full = lambda shape: pl.BlockSpec(shape, lambda b, c: (0, 0))
    bc_blk = 2 * SSD_INNER // SSD_BC_WIDTH
    dt_blk = (2 * SSD_INNER + SSD_BC_WIDTH) // LANES
    return pl.pallas_call(
        _ssd_kernel,
        grid=(batch, nc),
        in_specs=[
            pl.BlockSpec((q, SSD_INNER), lambda b, c: (tok(b, c), 0)),
            pl.BlockSpec((q, SSD_INNER), lambda b, c: (tok(b, c), 1)),
            pl.BlockSpec((q, SSD_BC_WIDTH), lambda b, c: (tok(b, c), bc_blk)),
            pl.BlockSpec((q, LANES), lambda b, c: (tok(b, c), dt_blk)),
            full((SSD_CONV, SSD_INNER)), full((SSD_CONV, SSD_BC_WIDTH)), full((1, SSD_INNER)), full((1, SSD_BC_WIDTH)),
            full((1, LANES)), full((1, LANES)), full((1, SSD_INNER)), full((1, SSD_INNER)),
        ],
        out_specs=pl.BlockSpec((q, SSD_INNER), lambda b, c: (tok(b, c), 0)),
        out_shape=jax.ShapeDtypeStruct((batch * seq, SSD_INNER), BF16),
        scratch_shapes=[
            pltpu.VMEM((SUBLANES + q, SSD_INNER), F32),
            pltpu.VMEM((SUBLANES + q, SSD_BC_WIDTH), F32),
            pltpu.VMEM((SSD_HEADS // 2, SSD_STATE, LANES), F32),
        ],
        compiler_params=_cparams("arbitrary", "arbitrary"),
        name="ssd_mixer",
    )(*args)


def _merge_kernel(ysw_ref, yss_ref, o0_ref, l0_ref, o1_ref, l1_ref, o2_ref, l2_ref, g0_ref, g1_ref, g2_ref,
                  p0_ref, p1_ref, p2_ref, out_ref):
    l0, l1, l2 = l0_ref[...], l1_ref[...], l2_ref[...]
    m = jnp.maximum(jnp.maximum(l0, l1), l2)
    e0, e1, e2 = jnp.exp(l0 - m), jnp.exp(l1 - m), jnp.exp(l2 - m)
    inv = 1.0 / (e0 + e1 + e2)
    ydil = (o0_ref[...] * (e0 * inv) + o1_ref[...] * (e1 * inv) + o2_ref[...] * (e2 * inv)).astype(BF16)
    merged = g0_ref[...] * jnp.dot(ysw_ref[...], p0_ref[...], preferred_element_type=F32)
    merged = merged + g1_ref[...] * jnp.dot(yss_ref[...], p1_ref[...], preferred_element_type=F32)
    merged = merged + g2_ref[...] * jnp.dot(ydil, p2_ref[...], preferred_element_type=F32)
    out_ref[...] = merged.astype(out_ref.dtype)


def _merge(y_swa, y_ssd, dil, gates, p_swa, p_ssd, p_dil, tm, tn):
    t = y_swa.shape[0]
    d = p_swa.shape[1]
    nj = d // tn
    rows = lambda width: pl.BlockSpec((tm, width), lambda i, j: (i, 0))
    gate = lambda k: pl.BlockSpec((tm, tn), lambda i, j: (i, k * nj + j))
    wcol = lambda width: pl.BlockSpec((width, tn), lambda i, j: (0, j))
    dil_args = [a for pair in dil for a in pair]
    return pl.pallas_call(
        _merge_kernel,
        grid=(t // tm, nj),
        in_specs=[rows(A_Q), rows(SSD_INNER)] + [rows(C_OUT)] * 6 + [gate(0), gate(1), gate(2)]
        + [wcol(A_Q), wcol(SSD_INNER), wcol(C_OUT)],
        out_specs=pl.BlockSpec((tm, tn), lambda i, j: (i, j)),
        out_shape=jax.ShapeDtypeStruct((t, d), BF16),
        compiler_params=_cparams("arbitrary", "arbitrary"),
        name="merge",
    )(y_swa, y_ssd, *dil_args, gates, gates, gates, p_swa, p_ssd, p_dil)


def _layer_norm(h, g, b):
    mu = jnp.mean(h, axis=-1, keepdims=True)
    hc = h - mu
    var = jnp.mean(hc * hc, axis=-1, keepdims=True)
    return hc * lax.rsqrt(var + LN_EPS) * g + b


LN1_ROW_SPLIT = 2


def _ln1_router_kernel(x_ref, m_ref, w_ref, g_ref, b_ref, wr_ref, br_ref, xo_ref, ids_ref, comb_ref, cnt_ref, *, alpha):
    i = pl.program_id(0)
    tm = x_ref.shape[0]
    rows = tm // LN1_ROW_SPLIT
    lane = lax.broadcasted_iota(I32, (rows, LANES), 1).astype(F32)
    big = float(2 * LANES)

    def first_argmax(vals):
        top = jnp.max(vals, axis=-1, keepdims=True)
        return top, jnp.min(jnp.where(vals == top, lane, big), axis=-1, keepdims=True)

    tops = []
    for part in range(LN1_ROW_SPLIT):
        sl = slice(part * rows, (part + 1) * rows)
        h = alpha * x_ref[sl, :] + jnp.dot(m_ref[sl, :], w_ref[...], preferred_element_type=F32)
        y = _layer_norm(h, g_ref[...], b_ref[...])
        xo_ref[sl, :] = y

        logits = jnp.dot(y.astype(BF16), wr_ref[...], preferred_element_type=F32) + br_ref[...]
        gl = jnp.where(lane < N_GROUPS, logits, NEG_INF)
        gmax, gsel = first_argmax(gl)
        p_group = 1.0 / jnp.sum(jnp.exp(gl - gmax), axis=-1, keepdims=True)
        lo = N_GROUPS + gsel * EXPERTS_PER_GROUP
        el = jnp.where((lane >= lo) & (lane < lo + EXPERTS_PER_GROUP), logits, NEG_INF)
        v1, i1 = first_argmax(el)
        v2, i2 = first_argmax(jnp.where(lane == i1, NEG_INF, el))
        e21 = jnp.exp(v2 - v1)
        inv = 1.0 / (1.0 + e21)
        comb_ref[sl, :] = jnp.where(lane == 0.0, inv * p_group, jnp.where(lane == 1.0, e21 * inv * p_group, 0.0))
        tops.append((i1, i2))

    @pl.when(i == 0)
    def _():
        cnt_ref[...] = jnp.zeros(cnt_ref.shape, F32)

    i1 = jnp.concatenate([t[0] for t in tops], axis=0)
    i2 = jnp.concatenate([t[1] for t in tops], axis=0)
    lane = lax.broadcasted_iota(I32, (tm, LANES), 1).astype(F32)
    pick1 = jnp.where(lane == i1, 1.0, 0.0)
    pick2 = jnp.where(lane == i2, 1.0, 0.0)
    picks = pick1 + pick2
    earlier = lax.broadcasted_iota(I32, (tm, tm), 0) > lax.broadcasted_iota(I32, (tm, tm), 1)
    before = jnp.dot(jnp.where(earlier, 1.0, 0.0).astype(BF16), picks.astype(BF16), preferred_element_type=F32) + cnt_ref[...]
    rank1 = jnp.sum(pick1 * before, axis=-1, keepdims=True)
    rank2 = jnp.sum(pick2 * before, axis=-1, keepdims=True)
    cnt_ref[...] = cnt_ref[...] + jnp.sum(picks, axis=0, keepdims=True)
    ids = jnp.where(lane == 0.0, i1 - N_GROUPS, jnp.where(lane == 1.0, i2 - N_GROUPS,
                    jnp.where(lane == 2.0, rank1, jnp.where(lane == 3.0, rank2, 0.0))))
    for c in range(tm // LANES):
        ids_ref[:, c * LANES:(c + 1) * LANES] = ids[c * LANES:(c + 1) * LANES, :].T[0:SUBLANES, :].astype(I32)


def _ln1_router(x, merged, w_out, g, b, wr, br, alpha, tm):
    t, d = x.shape
    rows = lambda width: pl.BlockSpec((tm, width), lambda i: (i, 0))
    full = lambda shape: pl.BlockSpec(shape, lambda i: (0, 0))
    once = lambda shape: pl.BlockSpec(shape, lambda i: (0, 0), pipeline_mode=pl.Buffered(1))
    return pl.pallas_call(
        functools.partial(_ln1_router_kernel, alpha=alpha),
        grid=(t // tm,),
        in_specs=[rows(d), rows(d), once((d, d)), full((1, d)), full((1, d)), once((d, LANES)), full((1, LANES))],
        out_specs=[rows(d), pl.BlockSpec((SUBLANES, tm), lambda i: (0, i)), rows(LANES), full((1, LANES))],
        out_shape=[jax.ShapeDtypeStruct((t, d), F32),
                   jax.ShapeDtypeStruct((SUBLANES, t), I32), jax.ShapeDtypeStruct((t, LANES), F32),
                   jax.ShapeDtypeStruct((1, LANES), F32)],
        compiler_params=_cparams("arbitrary"),
        name="ln1_router",
    )(x, merged, w_out, g.reshape(1, d), b.reshape(1, d), wr, br)


MOE_GATHER_AHEAD = 2
MOE_X_BUFFERS = 4


def _moe_kernel(be_ref, dest_ref, pad_lo_ref, pad_hi_ref, nu_ref, x_hbm, w1_ref, w3_ref, w2_ref, y_hbm,
                xbuf0, xbuf1, xbuf2, xbuf3, ybuf0, ybuf1, w1b, w3b, w2b, tok_ref, dst_ref, gsem, ssem, *, n_tok):
    b = pl.program_id(0)
    n_used = nu_ref[0]
    xbufs, ybufs = (xbuf0, xbuf1, xbuf2, xbuf3), (ybuf0, ybuf1)
    trash_row0 = TOP_K * n_tok

    def entry(blk, r):
        return (blk + 1) * MOE_BLOCK + r

    def gather_copy(blk, p, r):
        return pltpu.make_async_copy(
            x_hbm.at[pl.ds(tok_ref[entry(blk, r)], 1), :], xbufs[p].at[pl.ds(r, 1), :], gsem.at[p])

    def scatter_copy(blk, p, r):
        return pltpu.make_async_copy(
            ybufs[p].at[pl.ds(r, 1), :], y_hbm.at[pl.ds(dst_ref[entry(blk, r)], 1), :], ssem.at[p])

    def for_rows(fn):
        def chunk(c, carry):
            for u in range(MOE_DMA_UNROLL):
                fn(c * MOE_DMA_UNROLL + u)
            return carry
        lax.fori_loop(0, MOE_BLOCK // MOE_DMA_UNROLL, chunk, 0)

    @pl.when(b == 0)
    def _():
        def pad_entries(lo, hi):
            def pad_entry(s, carry):
                tok_ref[s] = 0
                dst_ref[s] = trash_row0 + lax.bitwise_and(s, MOE_BLOCK - 1)
                return carry
            lax.fori_loop(lo, hi, pad_entry, 0)

        pad_entries(0, MOE_BLOCK)
        lax.fori_loop(0, N_EXPERTS, lambda e, c: (pad_entries(entry(0, pad_lo_ref[e]), entry(0, pad_hi_ref[e])), c)[1], 0)
        pad_entries(entry(n_used, 0), entry(n_used + MOE_GATHER_AHEAD, 0))

        for k in range(TOP_K):
            def fill(c, carry):
                for u in range(MOE_DMA_UNROLL):
                    tkn = c * MOE_DMA_UNROLL + u
                    s = dest_ref[k * n_tok + tkn] + MOE_BLOCK
                    tok_ref[s] = tkn
                    dst_ref[s] = k * n_tok + tkn
                return carry
            lax.fori_loop(0, n_tok // MOE_DMA_UNROLL, fill, 0)

        ybuf1[...] = jnp.zeros(ybuf1.shape, F32)
        for ahead in range(MOE_GATHER_AHEAD):
            for_rows(lambda r: gather_copy(ahead, ahead, r).start())

    def step(p):
        py = p % 2
        qy = 1 - py
        px_next = (p + MOE_GATHER_AHEAD) % MOE_X_BUFFERS
        for_rows(lambda r: gather_copy(b, p, r).wait())

        @pl.when(b >= 1)
        def _():
            for_rows(lambda r: scatter_copy(b - 2, py, r).wait())

        @pl.when(jnp.logical_or(b == 0, be_ref[b] != be_ref[jnp.maximum(b - 1, 0)]))
        def _():
            w1b[...] = w1_ref[...].astype(BF16)
            w3b[...] = w3_ref[...].astype(BF16)
            w2b[...] = w2_ref[...].astype(BF16)

        def issue(part):
            def row_pair(i, carry):
                for priority in range(2):
                    r = 2 * i + priority
                    gather_copy(b + MOE_GATHER_AHEAD, px_next, r).start(priority=priority)
                    scatter_copy(b - 1, qy, r).start(priority=priority)
                return carry
            lax.fori_loop(part * MOE_BLOCK // 8, (part + 1) * MOE_BLOCK // 8, row_pair, 0, unroll=True)

        xb = xbufs[p][...].astype(BF16)
        issue(0)
        h1 = jnp.dot(xb, w1b[...], preferred_element_type=F32)
        issue(1)
        h3 = jnp.dot(xb, w3b[...], preferred_element_type=F32)
        issue(2)
        h = (h1 * _sigmoid(h1) * h3).astype(BF16)
        issue(3)
        ybufs[py][...] = jnp.dot(h, w2b[...], preferred_element_type=F32)

        @pl.when(b == n_used - 1)
        def _():
            for_rows(lambda r: scatter_copy(b - 1, qy, r).wait())
            for_rows(lambda r: scatter_copy(b, py, r).start())
            for_rows(lambda r: scatter_copy(b, py, r).wait())
            for ahead in range(1, MOE_GATHER_AHEAD + 1):
                for_rows(lambda r: gather_copy(b + ahead, (p + ahead) % MOE_X_BUFFERS, r).wait())

    for p in range(MOE_X_BUFFERS):
        pl.when(jnp.logical_and(b < n_used, b % MOE_X_BUFFERS == p))(functools.partial(step, p))


def _moe_experts(x1, plan, w1, w3, w2, layer):
    t, d = x1.shape
    block_e, dest, pad_lo, pad_hi, n_used = plan
    n_blocks = block_e.shape[0]
    table = pltpu.SMEM(((n_blocks + 1 + MOE_GATHER_AHEAD) * MOE_BLOCK,), I32)
    row_buf = pltpu.VMEM((MOE_BLOCK, d), F32)
    wspec = lambda shape: pl.BlockSpec((None, None) + shape, lambda b, be, *_: (layer, be[b], 0, 0))
    grid_spec = pltpu.PrefetchScalarGridSpec(
        num_scalar_prefetch=5,
        grid=(n_blocks,),
        in_specs=[pl.BlockSpec(memory_space=pl.ANY), wspec((d, D_EXPERT)), wspec((d, D_EXPERT)), wspec((D_EXPERT, d))],
        out_specs=pl.BlockSpec(memory_space=pl.ANY),
        scratch_shapes=[row_buf] * (MOE_X_BUFFERS + 2) + [
            pltpu.VMEM((d, D_EXPERT), BF16), pltpu.VMEM((d, D_EXPERT), BF16), pltpu.VMEM((D_EXPERT, d), BF16),
            table, table,
            pltpu.SemaphoreType.DMA((MOE_X_BUFFERS,)),
            pltpu.SemaphoreType.DMA((2,)),
        ],
    )
    return pl.pallas_call(
        functools.partial(_moe_kernel, n_tok=t),
        grid_spec=grid_spec,
        out_shape=jax.ShapeDtypeStruct((TOP_K * t + MOE_BLOCK, d), F32),
        compiler_params=_cparams("arbitrary"),
        name="moe_experts",
    )(block_e, dest, pad_lo, pad_hi, n_used, x1, w1, w3, w2)


def _ln2_kernel(x_ref, y0_ref, y1_ref, comb_ref, g_ref, b_ref, xo_ref, xb_ref, *, alpha):
    comb = comb_ref[...]
    h = alpha * x_ref[...] + comb[:, 0:1] * y0_ref[...] + comb[:, 1:2] * y1_ref[...]
    y = _layer_norm(h, g_ref[...], b_ref[...])
    xo_ref[...] = y
    xb_ref[...] = y.astype(BF16)


def _ln2(x1, y2, comb, g, b, alpha, tm):
    t, d = x1.shape
    nblk = t // tm
    rows = lambda width: pl.BlockSpec((tm, width), lambda i: (i, 0))
    full = lambda shape: pl.BlockSpec(shape, lambda i: (0, 0))
    return pl.pallas_call(
        functools.partial(_ln2_kernel, alpha=alpha),
        grid=(nblk,),
        in_specs=[rows(d), rows(d), pl.BlockSpec((tm, d), lambda i: (nblk + i, 0)), rows(LANES), full((1, d)), full((1, d))],
        out_specs=[rows(d), rows(d)],
        out_shape=[jax.ShapeDtypeStruct((t, d), F32), jax.ShapeDtypeStruct((t, d), BF16)],
        compiler_params=_cparams("arbitrary"),
        name="ln2",
    )(x1, y2, y2, comb, g.reshape(1, d), b.reshape(1, d))


def _dispatch_plan(ids, counts, n_tok):
    n_assign = n_tok * TOP_K
    n_blocks = -(-n_assign // MOE_BLOCK) + N_EXPERTS
    experts = jnp.arange(N_EXPERTS, dtype=I32)
    padded = (counts + MOE_BLOCK - 1) // MOE_BLOCK * MOE_BLOCK
    pad_end = jnp.cumsum(padded)
    pad_start = pad_end - padded
    slot0 = jnp.sum(jnp.where(ids[0:TOP_K, :, None] == experts, pad_start, 0), axis=-1)
    dest = (slot0 + ids[TOP_K:2 * TOP_K]).reshape(n_assign)
    starts = jnp.arange(n_blocks, dtype=I32) * MOE_BLOCK
    block_e = jnp.minimum(jnp.sum((pad_end[None, :] <= starts[:, None]).astype(I32), axis=1), N_EXPERTS - 1)
    return block_e, dest, pad_start + counts, pad_end, pad_end[-1:] // MOE_BLOCK


def _rope_tables(seq):
    inv_freq = 1.0 / (ROPE_THETA ** (jnp.arange(0, HEAD_DIM, 2, dtype=F32) / HEAD_DIM))
    ang = jnp.arange(seq, dtype=F32)[:, None] * inv_freq[None, :]
    cos, sin = jnp.cos(ang), jnp.sin(ang)
    reps = LANES // HEAD_DIM
    return jnp.tile(jnp.concatenate([cos, cos], axis=1), (1, reps)), jnp.tile(jnp.concatenate([-sin, sin], axis=1), (1, reps))


def _tile(total, want):
    return min(total, want)


def kernel(x, w_in, b_in, attn_sinks, conv_w, conv_b, dt_bias, a_log, d_skip, ssd_norm_w, proj_swa, proj_ssd, proj_dil, w_out, ln1_g, ln1_b, router_group_w, router_group_b, router_expert_w, router_expert_b, expert_w1, expert_w3, expert_w2, ln2_g, ln2_b):
    batch, seq, d = x.shape
    depth = w_in.shape[0]
    t = batch * seq
    alpha = (2 * depth) ** 0.25
    rope = _rope_tables(seq)
    tm = _tile(seq, 512)
    tmp = _tile(seq, 1024)

    sizes = (A_Q, A_KV, A_KV, SSD_INNER, SSD_INNER + SSD_BC_WIDTH, SSD_HEADS, C_QKV, C_QKV, C_QKV, 3 * d)
    offs = [0]
    for s in sizes:
        offs.append(offs[-1] + s)
    o_qa, o_ka, o_va, o_z, o_xbc, o_dt, o_qc, o_kc, o_vc, o_gate, o_end = offs

    n_a, n_misc, n_gate = o_z - o_qa, o_qc - o_z, o_end - o_gate
    dt_pad = LANES - SSD_HEADS

    def piece(lo, hi, pad=0):
        wp, bp = w_in[:, :, lo:hi], b_in[:, None, lo:hi]
        if pad:
            wp, bp = jnp.pad(wp, ((0, 0), (0, 0), (0, pad))), jnp.pad(bp, ((0, 0), (0, 0), (0, pad)))
        return wp.astype(BF16), bp

    w_a, b_a = piece(o_qa, o_z)
    w_misc, b_misc = piece(o_z, o_qc, dt_pad)
    w_c, b_c = piece(o_qc, o_gate)
    w_g, b_g = piece(o_gate, o_end)
    p_swa, p_ssd, p_dil, w_o = (m.astype(BF16) for m in (proj_swa, proj_ssd, proj_dil, w_out))
    wr_all = jnp.concatenate(
        [router_group_w, router_expert_w.transpose(0, 2, 1, 3).reshape(depth, d, N_EXPERTS)], axis=2)
    wr_all = jnp.pad(wr_all, ((0, 0), (0, 0), (0, LANES - N_GROUPS - N_EXPERTS))).astype(BF16)
    br_all = jnp.pad(jnp.concatenate([router_group_b, router_expert_b.reshape(depth, N_EXPERTS)], axis=1),
                     ((0, 0), (0, LANES - N_GROUPS - N_EXPERTS)))

    xf = x.reshape(t, d)
    xb = xf.astype(BF16)
    for l in range(depth):
        qkv_a = _proj(xb, w_a, b_a, l, 0, n_a, BF16, tmp, n_a, seq, rope=rope, rope_blocks=(A_Q + A_KV) // LANES,
                      dup_from=A_Q // LANES, name="proj_swa_qkv")
        qk_c = _proj(xb, w_c, b_c, l, 0, 2 * C_QKV, F32, tmp, C_QKV, seq, rope=rope, rope_blocks=C_QKV // LANES, name="proj_dil_qk")
        v_c = _proj(xb, w_c, b_c, l, 2 * C_QKV, C_QKV, F32, tmp, C_QKV, seq, name="proj_dil_v")
        misc = _proj(xb, w_misc, b_misc, l, 0, n_misc + dt_pad, F32, tmp, (n_misc + dt_pad) // 3, seq, name="proj_ssd")
        gates = _proj(xb, w_g, b_g, l, 0, n_gate, F32, tmp, 1024, seq, act="sigmoid", name="proj_gates")

        y_swa = _swa_attention(qkv_a, attn_sinks[l], batch, seq)
        y_ssd = _ssd_mixer(misc, conv_w[l], conv_b[l], dt_bias[l], a_log[l], d_skip[l], ssd_norm_w[l], batch, seq)
        dil = [_dilated_attention(qk_c, v_c, batch, seq, gi) for gi in range(len(DIL_CONFIGS))]

        merged = _merge(y_swa, y_ssd, dil, gates, p_swa[l], p_ssd[l], p_dil[l], tm, 1024)
        x1, ids, comb, cnt = _ln1_router(xf, merged, w_o[l], ln1_g[l], ln1_b[l], wr_all[l], br_all[l].reshape(1, LANES), alpha, tm)

        plan = _dispatch_plan(ids, cnt[0, N_GROUPS:N_GROUPS + N_EXPERTS].astype(I32), t)
        y2 = _moe_experts(x1, plan, expert_w1, expert_w3, expert_w2, l)
        xf, xb = _ln2(x1, y2, comb, ln2_g[l], ln2_b[l], alpha, tm)
    return xf.reshape(batch, seq, d)
```
